```python
import math
import jax, jax.numpy as jnp
from jax import lax
import numpy as np

D_MODEL = 4096
BATCH = 4
SEQ = 4096
DEPTH = 2

N_MEM = 256
EPS = 1e-6
NEG_INF = -1e30
HEAD_DIM_A = 128
WIDTH_A = 3 * D_MODEL // 8
N_HEADS_A = WIDTH_A // HEAD_DIM_A
ROT_DIM = HEAD_DIM_A // 4
ROPE_THETA = 500000.0
DILATION_PAIRS = ((128, 1), (512, 4), (2048, 16))
WIDTH_B = 3 * D_MODEL // 8
N_HEADS_B = 6
V_HEAD_B = WIDTH_B // N_HEADS_B
QK_HEAD_B = V_HEAD_B // 2
RET_CHUNK = 128
RET_THETA = 10000.0
WIDTH_C = D_MODEL - WIDTH_A - WIDTH_B
S5_GROUP = 16
S5_GROUPS = WIDTH_C // S5_GROUP
S5_STATE = 64
S5_DT_MIN = 1e-3
S5_DT_MAX = 1e-1
IN_WIDTHS = (WIDTH_A, WIDTH_A, WIDTH_A, N_HEADS_B * QK_HEAD_B, N_HEADS_B * QK_HEAD_B, WIDTH_B, WIDTH_B, WIDTH_C)
IN_WIDTH = sum(IN_WIDTHS)
IN_SPLITS = tuple(int(v) for v in np.cumsum(IN_WIDTHS)[:-1])
N_HEADS_X = 4
HEAD_DIM_X = D_MODEL // N_HEADS_X
N_EXPERTS = 16
EXPERT_FF = D_MODEL // 4
EC_CAPACITY_FACTOR = 2

kernel_name = 'hybrid_dilated_retention_s5_ec_encoder'


def rmsnorm(x, w):
    xf = x.astype(jnp.float32)
    y = xf * lax.rsqrt(jnp.mean(xf * xf, axis=-1, keepdims=True) + EPS)
    return (y * w.astype(jnp.float32)).astype(x.dtype)


def rotate(x, pos, freqs):
    half = freqs.shape[0]
    ang = pos.astype(jnp.float32)[:, :, None, None] * freqs
    cos, sin = jnp.cos(ang), jnp.sin(ang)
    xf = x.astype(jnp.float32)
    x1, x2, rest = xf[..., :half], xf[..., half:2 * half], xf[..., 2 * half:]
    return jnp.concatenate([x1 * cos - x2 * sin, x2 * cos + x1 * sin, rest], axis=-1)


def dilated_branch(q, k, v, dilation, radius):
    bn, s, nh, hd = q.shape
    blk = radius
    sub = s // dilation
    nb = -(-sub // blk)
    lp = nb * blk

    def regroup(t):
        return t.reshape(bn, sub, dilation, nh, hd).transpose(0, 2, 3, 1, 4)

    qg = jnp.pad(regroup(q), ((0, 0), (0, 0), (0, 0), (0, lp - sub), (0, 0)))
    qg = qg.reshape(bn, dilation, nh, nb, blk, hd)

    def band(t):
        tp = jnp.pad(regroup(t), ((0, 0), (0, 0), (0, 0), (blk, lp - sub + blk), (0, 0)))
        tp = tp.reshape(bn, dilation, nh, nb + 2, blk, hd)
        return jnp.concatenate([tp[:, :, :, 0:nb], tp[:, :, :, 1:nb + 1], tp[:, :, :, 2:nb + 2]], axis=4)

    kb, vb = band(k), band(v)
    jb = jnp.arange(nb)[:, None, None]
    qa = jnp.arange(blk)[None, :, None]
    ka = jnp.arange(3 * blk)[None, None, :]
    qi = jb * blk + qa
    ki = (jb - 1) * blk + ka
    mask = (jnp.abs(ki - qi) <= radius) & (ki >= 0) & (ki < sub)
    sc = jnp.einsum('bdhnqe,bdhnke->bdhnqk', qg, kb)
    sc = jnp.where(mask, sc, NEG_INF)
    m = jnp.max(sc, axis=-1, keepdims=True)
    p = jnp.exp(sc - m)
    den = jnp.sum(p, axis=-1, keepdims=True)
    o = jnp.einsum('bdhnqk,bdhnke->bdhnqe', p, vb) / den
    lse = (m + jnp.log(den))[..., 0]
    o = o.reshape(bn, dilation, nh, lp, hd)[:, :, :, :sub].transpose(0, 3, 1, 2, 4).reshape(bn, s, nh, hd)
    lse = lse.reshape(bn, dilation, nh, lp)[..., :sub].transpose(0, 3, 1, 2).reshape(bn, s, nh)
    return o, lse


def retention_direction(q, k, v, log_g, strict):
    bn, nh, s, dk = q.shape
    dv = v.shape[-1]
    cb = RET_CHUNK
    nc = s // cb
    qc = q.reshape(bn, nh, nc, cb, dk)
    kc = k.reshape(bn, nh, nc, cb, dk)
    vc = v.reshape(bn, nh, nc, cb, dv)
    i = jnp.arange(cb)[:, None]
    j = jnp.arange(cb)[None, :]
    keep = (i > j) if strict else (i >= j)
    diff = jnp.where(keep, i - j, 0).astype(jnp.float32)
    dmask = jnp.where(keep[None], jnp.exp(diff[None] * log_g[:, None, None]), 0.0)
    pos = jnp.arange(cb, dtype=jnp.float32)
    k_dec = jnp.exp((cb - 1 - pos)[None, :] * log_g[:, None])
    q_dec = jnp.exp((pos + 1)[None, :] * log_g[:, None])
    c_dec = jnp.exp(cb * log_g)
    scores = jnp.einsum('bhncd,bhnmd->bhncm', qc, kc) * dmask[None, :, None]
    intra = jnp.einsum('bhncm,bhnmv->bhncv', scores, vc)
    chunk_kv = jnp.einsum('bhnmd,hm,bhnmv->bhndv', kc, k_dec, vc)

    def step(state, kv_n):
        return state * c_dec[None, :, None, None] + kv_n, state

    _, prev = lax.scan(step, jnp.zeros((bn, nh, dk, dv), jnp.float32), jnp.moveaxis(chunk_kv, 2, 0))
    prev = jnp.moveaxis(prev, 0, 2)
    cross = jnp.einsum('bhncd,bhndv->bhncv', qc * q_dec[None, :, None, :, None], prev)
    return (intra + cross).reshape(bn, nh, s, dv)


def _cplx_combine(e1, e2):
    a1r, a1i, b1r, b1i = e1
    a2r, a2i, b2r, b2i = e2
    return (a2r * a1r - a2i * a1i,
            a2r * a1i + a2i * a1r,
            a2r * b1r - a2i * b1i + b2r,
            a2r * b1i + a2i * b1r + b2i)


def s5_direction(u, lam_re, lam_im, log_dt, b_re, b_im, c_re, c_im, reverse):
    s = u.shape[1]
    lam_re = lam_re.astype(jnp.float32)
    lam_im = lam_im.astype(jnp.float32)
    dt = jnp.exp(log_dt.astype(jnp.float32))[:, None]
    mag = jnp.exp(lam_re * dt)
    ar = mag * jnp.cos(lam_im * dt)
    ai = mag * jnp.sin(lam_im * dt)
    den = lam_re * lam_re + lam_im * lam_im
    nr = ar - 1.0
    cr = (nr * lam_re + ai * lam_im) / den
    ci = (ai * lam_re - nr * lam_im) / den
    b_re = b_re.astype(jnp.float32)
    b_im = b_im.astype(jnp.float32)
    bbr = cr[..., None] * b_re - ci[..., None] * b_im
    bbi = cr[..., None] * b_im + ci[..., None] * b_re
    xr = jnp.einsum('gpc,bsgc->bsgp', bbr, u)
    xi = jnp.einsum('gpc,bsgc->bsgp', bbi, u)
    ar_s = jnp.broadcast_to(ar, (s,) + ar.shape)
    ai_s = jnp.broadcast_to(ai, (s,) + ai.shape)

    def scan_one(br, bi):
        out = lax.associative_scan(_cplx_combine, (ar_s, ai_s, br, bi), reverse=reverse, axis=0)
        return out[2], out[3]

    hr, hi = jax.vmap(scan_one)(xr, xi)
    return (jnp.einsum('gcp,bsgp->bsgc', c_re.astype(jnp.float32), hr)
            - jnp.einsum('gcp,bsgp->bsgc', c_im.astype(jnp.float32), hi))


def setup_inputs(seed: int = 0) -> dict:
    key = jax.random.key(seed)
    ks = jax.random.split(key, 28)
    f32 = jnp.float32

    def normal(k, shape, scale):
        return jax.random.normal(k, shape, f32) * scale

    def gain(k, shape):
        return 1.0 + 0.02 * jax.random.normal(k, shape, f32)

    x = normal(ks[0], (BATCH, SEQ, D_MODEL), 1.0)
    mem = normal(ks[1], (BATCH, N_MEM, D_MODEL), 1.0)
    positions = (jax.random.randint(ks[2], (BATCH, 1), 0, 1024, dtype=jnp.int32)
                 + jnp.arange(SEQ, dtype=jnp.int32)[None, :])
    w_in = normal(ks[3], (DEPTH, D_MODEL, IN_WIDTH), D_MODEL ** -0.5)
    w_out = normal(ks[4], (DEPTH, D_MODEL, D_MODEL), D_MODEL ** -0.5)
    norm_mix_w = gain(ks[5], (DEPTH, D_MODEL))
    norm_cross_w = gain(ks[6], (DEPTH, D_MODEL))
    norm_mem_w = gain(ks[7], (DEPTH, D_MODEL))
    norm_ffn_w = gain(ks[8], (DEPTH, D_MODEL))
    final_norm_w = gain(ks[9], (D_MODEL,))
    ret_logit = np.log(2.0 ** (5.0 + np.arange(N_HEADS_B)) - 1.0).astype(np.float32)
    ret_decay = jnp.asarray(ret_logit)[None, None, :] + normal(ks[10], (DEPTH, 2, N_HEADS_B), 0.05)
    ret_gn_w = gain(ks[11], (DEPTH, WIDTH_B))
    s5_lam_re = -0.5 + normal(ks[12], (DEPTH, 2, S5_GROUPS, S5_STATE), 0.01)
    s5_lam_im = math.pi * jnp.arange(S5_STATE, dtype=f32) + normal(ks[13], (DEPTH, 2, S5_GROUPS, S5_STATE), 0.01)
    s5_log_dt = jax.random.uniform(ks[14], (DEPTH, 2, S5_GROUPS), f32, math.log(S5_DT_MIN), math.log(S5_DT_MAX))
    s5_b_re = normal(ks[15], (DEPTH, 2, S5_GROUPS, S5_STATE, S5_GROUP), (2 * S5_GROUP) ** -0.5)
    s5_b_im = normal(ks[16], (DEPTH, 2, S5_GROUPS, S5_STATE, S5_GROUP), (2 * S5_GROUP) ** -0.5)
    s5_c_re = normal(ks[17], (DEPTH, 2, S5_GROUPS, S5_GROUP, S5_STATE), S5_STATE ** -0.5)
    s5_c_im = normal(ks[18], (DEPTH, 2, S5_GROUPS, S5_GROUP, S5_STATE), S5_STATE ** -0.5)
    s5_d = normal(ks[19], (DEPTH, WIDTH_C), 1.0)
    s5_glu_w = normal(ks[20], (DEPTH, WIDTH_C, WIDTH_C), WIDTH_C ** -0.5)
    cross_wq = normal(ks[21], (DEPTH, D_MODEL, D_MODEL), D_MODEL ** -0.5)
    cross_wkv = normal(ks[22], (DEPTH, D_MODEL, 2 * D_MODEL), D_MODEL ** -0.5)
    cross_wo = normal(ks[23], (DEPTH, D_MODEL, D_MODEL), D_MODEL ** -0.5)
    router_w = normal(ks[24], (DEPTH, D_MODEL, N_EXPERTS), D_MODEL ** -0.5)
    expert_w_gate = normal(ks[25], (DEPTH, N_EXPERTS, D_MODEL, EXPERT_FF), D_MODEL ** -0.5)
    expert_w_up = normal(ks[26], (DEPTH, N_EXPERTS, D_MODEL, EXPERT_FF), D_MODEL ** -0.5)
    expert_w_down = normal(ks[27], (DEPTH, N_EXPERTS, EXPERT_FF, D_MODEL), EXPERT_FF ** -0.5)
    return {'x': x, 'mem': mem, 'positions': positions, 'w_in': w_in, 'w_out': w_out,
            'norm_mix_w': norm_mix_w, 'norm_cross_w': norm_cross_w, 'norm_mem_w': norm_mem_w,
            'norm_ffn_w': norm_ffn_w, 'final_norm_w': final_norm_w, 'ret_decay': ret_decay,
            'ret_gn_w': ret_gn_w, 's5_lam_re': s5_lam_re, 's5_lam_im': s5_lam_im, 's5_log_dt': s5_log_dt,
            's5_b_re': s5_b_re, 's5_b_im': s5_b_im, 's5_c_re': s5_c_re, 's5_c_im': s5_c_im,
            's5_d': s5_d, 's5_glu_w': s5_glu_w, 'cross_wq': cross_wq, 'cross_wkv': cross_wkv,
            'cross_wo': cross_wo, 'router_w': router_w, 'expert_w_gate': expert_w_gate,
            'expert_w_up': expert_w_up, 'expert_w_down': expert_w_down}


def reference(x, mem, positions, w_in, w_out, norm_mix_w, norm_cross_w, norm_mem_w, norm_ffn_w,
              final_norm_w, ret_decay, ret_gn_w, s5_lam_re, s5_lam_im, s5_log_dt, s5_b_re, s5_b_im,
              s5_c_re, s5_c_im, s5_d, s5_glu_w, cross_wq, cross_wkv, cross_wo, router_w,
              expert_w_gate, expert_w_up, expert_w_down):
    bn, s, _ = x.shape
    dt = x.dtype
    rope_freqs = ROPE_THETA ** (-(jnp.arange(ROT_DIM // 2, dtype=jnp.float32) * 2.0 / ROT_DIM))
    ret_freqs = RET_THETA ** (-jnp.linspace(0.0, 1.0, QK_HEAD_B // 2, dtype=jnp.float32))
    capacity = EC_CAPACITY_FACTOR * s // N_EXPERTS
    bidx = jnp.arange(bn)[:, None, None]
    for l in range(DEPTH):
        h = rmsnorm(x, norm_mix_w[l])
        proj = h @ w_in[l]
        qa, ka, va, qb, kb, vb, gb, uc = jnp.split(proj, IN_SPLITS, axis=-1)

        qa = rotate(qa.reshape(bn, s, N_HEADS_A, HEAD_DIM_A), positions, rope_freqs) * (HEAD_DIM_A ** -0.5)
        ka = rotate(ka.reshape(bn, s, N_HEADS_A, HEAD_DIM_A), positions, rope_freqs)
        va = va.reshape(bn, s, N_HEADS_A, HEAD_DIM_A).astype(jnp.float32)
        outs, lses = [], []
        for window, dil in DILATION_PAIRS:
            o_b, lse_b = dilated_branch(qa, ka, va, dil, window // (2 * dil))
            outs.append(o_b)
            lses.append(lse_b)
        wts = jax.nn.softmax(jnp.stack(lses, axis=-1), axis=-1)
        a_out = jnp.einsum('bshr,bshre->bshe', wts, jnp.stack(outs, axis=3)).reshape(bn, s, WIDTH_A).astype(dt)

        qr = rotate(qb.reshape(bn, s, N_HEADS_B, QK_HEAD_B), positions, ret_freqs).transpose(0, 2, 1, 3)
        kr = (rotate(kb.reshape(bn, s, N_HEADS_B, QK_HEAD_B), positions, ret_freqs)
              * (QK_HEAD_B ** -0.5)).transpose(0, 2, 1, 3)
        vr = vb.reshape(bn, s, N_HEADS_B, V_HEAD_B).astype(jnp.float32).transpose(0, 2, 1, 3)
        log_g = jax.nn.log_sigmoid(ret_decay[l].astype(jnp.float32))
        r_fwd = retention_direction(qr, kr, vr, log_g[0], strict=False)
        r_bwd = jnp.flip(retention_direction(jnp.flip(qr, 2), jnp.flip(kr, 2), jnp.flip(vr, 2), log_g[1], strict=True), 2)
        r = (r_fwd + r_bwd).transpose(0, 2, 1, 3)
        mu = jnp.mean(r, axis=-1, keepdims=True)
        var = jnp.mean(jnp.square(r - mu), axis=-1, keepdims=True)
        r = (r - mu) * lax.rsqrt(var + EPS) * ret_gn_w[l].astype(jnp.float32).reshape(N_HEADS_B, V_HEAD_B)
        b_out = (r.reshape(bn, s, WIDTH_B) * jax.nn.silu(gb.astype(jnp.float32))).astype(dt)

        u = uc.astype(jnp.float32).reshape(bn, s, S5_GROUPS, S5_GROUP)
        y = (s5_direction(u, s5_lam_re[l, 0], s5_lam_im[l, 0], s5_log_dt[l, 0], s5_b_re[l, 0], s5_b_im[l, 0],
                          s5_c_re[l, 0], s5_c_im[l, 0], reverse=False)
             + s5_direction(u, s5_lam_re[l, 1], s5_lam_im[l, 1], s5_log_dt[l, 1], s5_b_re[l, 1], s5_b_im[l, 1],
                            s5_c_re[l, 1], s5_c_im[l, 1], reverse=True)
             + u * s5_d[l].astype(jnp.float32).reshape(S5_GROUPS, S5_GROUP))
        y = jax.nn.gelu(y.reshape(bn, s, WIDTH_C))
        c_out = (y * jax.nn.sigmoid(y @ s5_glu_w[l].astype(jnp.float32))).astype(dt)

        x = x + jnp.concatenate([a_out, b_out, c_out], axis=-1) @ w_out[l]

        h = rmsnorm(x, norm_cross_w[l])
        mn = rmsnorm(mem, norm_mem_w[l])
        q = (h @ cross_wq[l]).reshape(bn, s, N_HEADS_X, HEAD_DIM_X).astype(jnp.float32)
        kv = mn @ cross_wkv[l]
        km = kv[..., :D_MODEL].reshape(bn, N_MEM, N_HEADS_X, HEAD_DIM_X).astype(jnp.float32)
        vm = kv[..., D_MODEL:].reshape(bn, N_MEM, N_HEADS_X, HEAD_DIM_X).astype(jnp.float32)
        p = jax.nn.softmax(jnp.einsum('bshd,bmhd->bhsm', q, km) * (HEAD_DIM_X ** -0.5), axis=-1)
        o = jnp.einsum('bhsm,bmhd->bshd', p, vm).reshape(bn, s, D_MODEL).astype(dt)
        x = x + o @ cross_wo[l]

        h = rmsnorm(x, norm_ffn_w[l])
        aff = jax.nn.softmax((h @ router_w[l]).astype(jnp.float32), axis=-1)
        gate, idx = lax.top_k(aff.transpose(0, 2, 1), capacity)
        xe = jax.vmap(lambda hb, ib: hb[ib])(h, idx)
        g = jnp.einsum('becd,edf->becf', xe, expert_w_gate[l])
        up = jnp.einsum('becd,edf->becf', xe, expert_w_up[l])
        ye = jnp.einsum('becf,efd->becd', jax.nn.silu(g) * up, expert_w_down[l]) * gate[..., None].astype(dt)
        x = x + jnp.zeros_like(x).at[bidx, idx].add(ye)
    return rmsnorm(x, final_norm_w)
```

```python
import functools
import math

import numpy as np
import jax
import jax.numpy as jnp
from jax import lax
from jax.experimental import pallas as pl
from jax.experimental.pallas import tpu as pltpu

F32 = jnp.float32
BF16 = jnp.bfloat16

EPS = 1e-6
NEG_INF = -1e30
VMEM_LIMIT_BYTES = 56 * 1024 * 1024
LANES = 128

HEAD_DIM_A = 128
ROT_DIM_A = HEAD_DIM_A // 4
ROPE_THETA = 500000.0
DILATION_PAIRS = ((128, 1), (512, 4), (2048, 16))
ATT_BLK = 128
N_HEADS_B = 6
RET_CHUNK = 128
RET_THETA = 10000.0
S5_GROUP = 16
S5_STATE = 64
S5_CHUNK = 32
N_HEADS_X = 4
N_EXPERTS = 16
EC_CAPACITY_FACTOR = 2


def _cparams(*sem):
    return pltpu.CompilerParams(dimension_semantics=sem, vmem_limit_bytes=VMEM_LIMIT_BYTES)


def _sigmoid(x):
    return 1.0 / (1.0 + jnp.exp(-x))


def _rmsnorm_rows(x, w):
    ms = jnp.mean(x * x, axis=-1, keepdims=True)
    return x * lax.rsqrt(ms + EPS) * w


def _rmsnorm_kernel(x_ref, w_ref, o_ref):
    o_ref[...] = _rmsnorm_rows(x_ref[...], w_ref[...]).astype(o_ref.dtype)


def _rmsnorm(x2, w, out_dtype, tm=512):
    t, d = x2.shape
    tm = min(tm, t)
    return pl.pallas_call(
        _rmsnorm_kernel,
        grid=(t // tm,),
        in_specs=[pl.BlockSpec((tm, d), lambda i: (i, 0)), pl.BlockSpec((1, d), lambda i: (0, 0))],
        out_specs=pl.BlockSpec((tm, d), lambda i: (i, 0)),
        out_shape=jax.ShapeDtypeStruct((t, d), out_dtype),
        compiler_params=_cparams("parallel"),
        name="rmsnorm",
    )(x2, w.reshape(1, d))


def _router_kernel(x_ref, w_ref, rw_ref, lg_ref):
    h = _rmsnorm_rows(x_ref[...], w_ref[...]).astype(BF16)
    lg_ref[...] = lax.dot_general(rw_ref[...], h, (((1,), (1,)), ((), ())), preferred_element_type=F32)


def _router_logits(x2, w, rw_t, tm=512):
    t, d = x2.shape
    ne = rw_t.shape[0]
    return pl.pallas_call(
        _router_kernel,
        grid=(t // tm,),
        in_specs=[
            pl.BlockSpec((tm, d), lambda i: (i, 0)),
            pl.BlockSpec((1, d), lambda i: (0, 0)),
            pl.BlockSpec((ne, d), lambda i: (0, 0)),
        ],
        out_specs=pl.BlockSpec((ne, tm), lambda i: (0, i)),
        out_shape=jax.ShapeDtypeStruct((ne, t), F32),
        compiler_params=_cparams("parallel"),
        name="router_logits",
    )(x2, w.reshape(1, d), rw_t)


def _mm_kernel(a_ref, w_ref, o_ref):
    o_ref[...] = jnp.dot(a_ref[...], w_ref[...], preferred_element_type=F32).astype(o_ref.dtype)


def _mm_res_kernel(a_ref, w_ref, r_ref, o_ref):
    o_ref[...] = r_ref[...] + jnp.dot(a_ref[...], w_ref[...], preferred_element_type=F32)


def _mm3_res_kernel(a1_ref, a2_ref, a3_ref, w1_ref, w2_ref, w3_ref, r_ref, o_ref):
    acc = jnp.dot(a1_ref[...], w1_ref[...], preferred_element_type=F32)
    acc += jnp.dot(a2_ref[...], w2_ref[...], preferred_element_type=F32)
    acc += jnp.dot(a3_ref[...], w3_ref[...], preferred_element_type=F32)
    o_ref[...] = r_ref[...] + acc


def _matmul(a, w, out_dtype, res=None, tm=1024, tn=512):
    m, k = a.shape
    n = w.shape[1]
    tm = min(tm, m)
    tn = min(tn, n)
    in_specs = [pl.BlockSpec((tm, k), lambda i, j: (i, 0)), pl.BlockSpec((k, tn), lambda i, j: (0, j))]
    args = [a, w]
    kern = _mm_kernel
    if res is not None:
        in_specs.append(pl.BlockSpec((tm, tn), lambda i, j: (i, j)))
        args.append(res)
        kern = _mm_res_kernel
    return pl.pallas_call(
        kern,
        grid=(m // tm, n // tn),
        in_specs=in_specs,
        out_specs=pl.BlockSpec((tm, tn), lambda i, j: (i, j)),
        out_shape=jax.ShapeDtypeStruct((m, n), out_dtype),
        compiler_params=_cparams("parallel", "parallel"),
        name="matmul",
    )(*args)


def _out_proj(a1, a2, a3, w, res, tm=1024, tn=512):
    m = a1.shape[0]
    k1, k2, k3 = a1.shape[1], a2.shape[1], a3.shape[1]
    n = w.shape[1]
    assert k1 == k2 and (k1 + k2) % k3 == 0
    return pl.pallas_call(
        _mm3_res_kernel,
        grid=(m // tm, n // tn),
        in_specs=[
            pl.BlockSpec((tm, k1), lambda i, j: (i, 0)),
            pl.BlockSpec((tm, k2), lambda i, j: (i, 0)),
            pl.BlockSpec((tm, k3), lambda i, j: (i, 0)),
            pl.BlockSpec((k1, tn), lambda i, j: (0, j)),
            pl.BlockSpec((k2, tn), lambda i, j: (1, j)),
            pl.BlockSpec((k3, tn), lambda i, j: ((k1 + k2) // k3, j)),
            pl.BlockSpec((tm, tn), lambda i, j: (i, j)),
        ],
        out_specs=pl.BlockSpec((tm, tn), lambda i, j: (i, j)),
        out_shape=jax.ShapeDtypeStruct((m, n), F32),
        compiler_params=_cparams("parallel", "parallel"),
        name="out_proj",
    )(a1, a2, a3, w, w, w, res)


def _rope_kernel(pos_ref, fa_ref, fb_ref, ca_ref, sa_ref, cb_ref, sb_ref):
    pos = pos_ref[...].astype(F32)
    ang_a = pos * fa_ref[...]
    ca_ref[...] = jnp.cos(ang_a)
    sa_ref[...] = jnp.sin(ang_a)
    ang_b = pos * fb_ref[...]
    lane = lax.broadcasted_iota(jnp.int32, ang_b.shape, 1)
    sin_b = jnp.sin(ang_b)
    cb_ref[...] = jnp.cos(ang_b)
    sb_ref[...] = jnp.where(lane < LANES // 2, -sin_b, sin_b)


def _rope_tables(positions):
    b, s = positions.shape
    half_a = ROT_DIM_A // 2
    fa = ROPE_THETA ** (-(jnp.arange(half_a, dtype=F32) * 2.0 / ROT_DIM_A))
    fa_row = jnp.concatenate([fa, fa, jnp.zeros((LANES - 2 * half_a,), F32)]).reshape(1, LANES)
    fb = RET_THETA ** (-jnp.linspace(0.0, 1.0, LANES // 2, dtype=F32))
    fb_row = jnp.concatenate([fb, fb]).reshape(1, LANES)
    tab = jax.ShapeDtypeStruct((b, s, LANES), F32)
    row = pl.BlockSpec((1, LANES), lambda i: (0, 0))
    out = pl.BlockSpec((None, s, LANES), lambda i: (i, 0, 0))
    return pl.pallas_call(
        _rope_kernel,
        grid=(b,),
        in_specs=[pl.BlockSpec((None, s, 1), lambda i: (i, 0, 0)), row, row],
        out_specs=[out, out, out, out],
        out_shape=[tab, tab, tab, tab],
        compiler_params=_cparams("parallel"),
        name="rope_tables",
    )(positions.reshape(b, s, 1), fa_row, fb_row)


def _attn_a_kernel(q_ref, k_ref, v_ref, c_ref, s_ref, o_ref,
                   qn, kn, qs, ks, vs, accr, mr, lr, accn, mn, ln, *, seq, radius):
    blk = ATT_BLK
    win = blk + 2 * radius
    rc = 128
    scale = HEAD_DIM_A ** -0.5
    half = ROT_DIM_A // 2

    def rot_body(i, carry):
        rows = pl.ds(pl.multiple_of(i * rc, rc), rc)
        lane = lax.broadcasted_iota(jnp.int32, (rc, LANES), 1)
        c = c_ref[rows, :]
        s = s_ref[rows, :]
        s_lo = jnp.where(lane < half, -s, 0.0)
        s_hi = jnp.where(lane >= half, s, 0.0)
        q = q_ref[rows, :]
        k = k_ref[rows, :]
        qn[rows, :] = (q * c + pltpu.roll(q, LANES - half, 1) * s_lo + pltpu.roll(q, half, 1) * s_hi) * scale
        kn[rows, :] = k * c + pltpu.roll(k, LANES - half, 1) * s_lo + pltpu.roll(k, half, 1) * s_hi
        accn[rows, :] = jnp.zeros((rc, LANES), F32)
        mn[rows, :] = jnp.full((rc, LANES), NEG_INF, F32)
        ln[rows, :] = jnp.zeros((rc, LANES), F32)
        return carry

    lax.fori_loop(0, seq // rc, rot_body, 0)

    dmat = (lax.broadcasted_iota(jnp.int32, (blk, win), 1)
            - lax.broadcasted_iota(jnp.int32, (blk, win), 0))

    for _, dil in DILATION_PAIRS:
        sub = seq // dil
        nbr = sub // blk

        def nat_rows(r, cc, dil=dil):
            if dil == 1:
                return pl.ds(cc * rc, rc)
            return pl.ds(r + dil * rc * cc, rc, stride=dil)

        for r in range(dil):
            for cc in range(sub // rc):
                dst = pl.ds(r * sub + cc * rc, rc)
                src = nat_rows(r, cc)
                qs[dst, :] = qn[src, :].astype(BF16)
                ks[dst, :] = kn[src, :].astype(BF16)
                vs[dst, :] = v_ref[src, :].astype(BF16)

        def blk_body(n, carry, sub=sub, nbr=nbr):
            r = n // nbr
            j = n - r * nbr
            q0 = pl.multiple_of(n * blk, blk)
            wrel = jnp.clip(j * blk - radius, 0, sub - win)
            ws = pl.multiple_of(r * sub + wrel, radius)
            qb = qs[pl.ds(q0, blk), :]
            kw = ks[pl.ds(ws, win), :]
            vw = vs[pl.ds(ws, win), :]
            sc = lax.dot_general(qb, kw, (((1,), (1,)), ((), ())), preferred_element_type=F32)
            sc = jnp.where(jnp.abs(dmat + (wrel - j * blk)) <= radius, sc, NEG_INF)
            m = jnp.max(sc, axis=1, keepdims=True)
            p = jnp.exp(sc - m)
            l = jnp.sum(p, axis=1, keepdims=True)
            acc = jnp.dot(p.astype(BF16), vw, preferred_element_type=F32)
            accr[pl.ds(q0, blk), :] = acc
            mr[pl.ds(q0, blk), :] = jnp.broadcast_to(m, (blk, LANES))
            lr[pl.ds(q0, blk), :] = jnp.broadcast_to(l, (blk, LANES))
            return carry

        lax.fori_loop(0, seq // blk, blk_body, 0, unroll=2)

        for r in range(dil):
            for cc in range(sub // rc):
                nat = nat_rows(r, cc)
                reg = pl.ds(r * sub + cc * rc, rc)
                m_old = mn[nat, :]
                m_new = mr[reg, :]
                m_max = jnp.maximum(m_old, m_new)
                e_old = jnp.exp(m_old - m_max)
                e_new = jnp.exp(m_new - m_max)
                accn[nat, :] = accn[nat, :] * e_old + accr[reg, :] * e_new
                ln[nat, :] = ln[nat, :] * e_old + lr[reg, :] * e_new
                mn[nat, :] = m_max

    def out_body(i, carry):
        rows = pl.ds(pl.multiple_of(i * rc, rc), rc)
        o_ref[rows, :] = (accn[rows, :] / ln[rows, :]).astype(o_ref.dtype)
        return carry

    lax.fori_loop(0, seq // rc, out_body, 0)


def _attn_a(proj3, ca, sa, n_heads):
    b, s, _ = proj3.shape
    radii = {w // (2 * d) for w, d in DILATION_PAIRS}
    assert len(radii) == 1
    radius = radii.pop()
    for _, d in DILATION_PAIRS:
        assert s % (d * ATT_BLK) == 0 and s // d >= ATT_BLK + 2 * radius

    def col(off):
        return pl.BlockSpec((None, s, HEAD_DIM_A), lambda i, h: (i, 0, off + h))

    tab = pl.BlockSpec((None, s, LANES), lambda i, h: (i, 0, 0))
    f32s = lambda: pltpu.VMEM((s, LANES), F32)
    bf16s = lambda: pltpu.VMEM((s, LANES), BF16)
    return pl.pallas_call(
        functools.partial(_attn_a_kernel, seq=s, radius=radius),
        grid=(b, n_heads),
        in_specs=[col(0), col(n_heads), col(2 * n_heads), tab, tab],
        out_specs=pl.BlockSpec((None, s, HEAD_DIM_A), lambda i, h: (i, 0, h)),
        out_shape=jax.ShapeDtypeStruct((b, s, n_heads * HEAD_DIM_A), BF16),
        scratch_shapes=[f32s(), f32s(), bf16s(), bf16s(), bf16s(),
                        f32s(), f32s(), f32s(), f32s(), f32s(), f32s()],
        compiler_params=_cparams("parallel", "parallel"),
        name="attn_a",
    )(proj3, proj3, proj3, ca, sa)


def _ret_kernel(q_ref, k_ref, v_ref, g_ref, c_ref, s_ref, dm_ref, dec_ref, cdec_ref, gn_ref, o_ref,
                qn, kn, racc, sf, sb, *, seq):
    cb = RET_CHUNK
    nc = seq // cb
    dk = q_ref.shape[-1]
    dv = v_ref.shape[-1]
    h = pl.program_id(1)
    kscale = dk ** -0.5

    def rot_body(i, carry):
        rows = pl.ds(pl.multiple_of(i * cb, cb), cb)
        c = c_ref[rows, :]
        s = s_ref[rows, :]
        q = q_ref[rows, :]
        k = k_ref[rows, :]
        qn[rows, :] = q * c + pltpu.roll(q, dk // 2, 1) * s
        kn[rows, :] = (k * c + pltpu.roll(k, dk // 2, 1) * s) * kscale
        racc[rows, :] = jnp.zeros((cb, dv), F32)
        return carry

    lax.fori_loop(0, nc, rot_body, 0)
    sf[...] = jnp.zeros(sf.shape, F32)
    sb[...] = jnp.zeros(sb.shape, F32)
    c_f = cdec_ref[h, 0]
    c_b = cdec_ref[h, 1]

    def body(n, carry):
        rows_f = pl.ds(pl.multiple_of(n * cb, cb), cb)
        rows_b = pl.ds(pl.multiple_of((nc - 1 - n) * cb, cb), cb)
        q = qn[rows_f, :]
        k = kn[rows_f, :]
        v = v_ref[rows_f, :].astype(BF16)
        sc = lax.dot_general(q.astype(BF16), k.astype(BF16), (((1,), (1,)), ((), ())),
                             preferred_element_type=F32) * dm_ref[...]
        intra = jnp.dot(sc.astype(BF16), v, preferred_element_type=F32)
        cross = jnp.dot((q * dec_ref[0]).astype(BF16), sf[...].astype(BF16), preferred_element_type=F32)
        racc[rows_f, :] += intra + cross
        kd = (k * dec_ref[1]).T.astype(BF16)
        sf[...] = c_f * sf[...] + jnp.dot(kd, v, preferred_element_type=F32)
        q2 = qn[rows_b, :]
        k2 = kn[rows_b, :]
        v2 = v_ref[rows_b, :].astype(BF16)
        cross_b = jnp.dot((q2 * dec_ref[2]).astype(BF16), sb[...].astype(BF16), preferred_element_type=F32)
        racc[rows_b, :] += cross_b
        kd2 = (k2 * dec_ref[3]).T.astype(BF16)
        sb[...] = c_b * sb[...] + jnp.dot(kd2, v2, preferred_element_type=F32)
        return carry

    lax.fori_loop(0, nc, body, 0)

    def out_body(i, carry):
        rows = pl.ds(pl.multiple_of(i * cb, cb), cb)
        r = racc[rows, :]
        mu = jnp.mean(r, axis=-1, keepdims=True)
        xc = r - mu
        var = jnp.mean(xc * xc, axis=-1, keepdims=True)
        y = xc * lax.rsqrt(var + EPS) * gn_ref[...]
        g = g_ref[rows, :]
        o_ref[rows, :] = (y * (g * _sigmoid(g))).astype(o_ref.dtype)
        return carry

    lax.fori_loop(0, nc, out_body, 0)


def _ret_tables(ret_decay_l):
    cb = RET_CHUNK
    log_g = jax.nn.log_sigmoid(ret_decay_l.astype(F32))
    lf = log_g[0][:, None, None]
    lb = log_g[1][:, None, None]
    i = jnp.arange(cb)[:, None]
    j = jnp.arange(cb)[None, :]
    dfwd = jnp.where(i >= j, i - j, 0).astype(F32)[None]
    dbwd = jnp.where(j > i, j - i, 0).astype(F32)[None]
    dmask = jnp.where((i >= j)[None], jnp.exp(dfwd * lf), jnp.exp(dbwd * lb))
    pos = jnp.arange(cb, dtype=F32)[None, :]
    lf2, lb2 = log_g[0][:, None], log_g[1][:, None]
    dec = jnp.stack([jnp.exp((pos + 1.0) * lf2), jnp.exp((cb - 1.0 - pos) * lf2),
                     jnp.exp((cb - pos) * lb2), jnp.exp(pos * lb2)], axis=1)
    dec = jnp.broadcast_to(dec[..., None], dec.shape + (LANES,))
    cdec = jnp.exp(cb * log_g).T
    return dmask, dec, cdec


def _retention(proj3, cbt, sbt, ret_decay_l, gn_w, q_off, n_heads):
    b, s, _ = proj3.shape
    dk = LANES
    dv = 2 * dk
    dmask, dec, cdec = _ret_tables(ret_decay_l)
    qb0 = q_off // dk
    kb0 = qb0 + n_heads
    vb0 = (q_off + 2 * n_heads * dk) // dv
    gb0 = vb0 + n_heads
    tab = pl.BlockSpec((None, s, LANES), lambda i, h: (i, 0, 0))
    return pl.pallas_call(
        functools.partial(_ret_kernel, seq=s),
        grid=(b, n_heads),
        in_specs=[
            pl.BlockSpec((None, s, dk), lambda i, h: (i, 0, qb0 + h)),
            pl.BlockSpec((None, s, dk), lambda i, h: (i, 0, kb0 + h)),
            pl.BlockSpec((None, s, dv), lambda i, h: (i, 0, vb0 + h)),
            pl.BlockSpec((None, s, dv), lambda i, h: (i, 0, gb0 + h)),
            tab, tab,
            pl.BlockSpec((None, RET_CHUNK, RET_CHUNK), lambda i, h: (h, 0, 0)),
            pl.BlockSpec((None, 4, RET_CHUNK, LANES), lambda i, h: (h, 0, 0, 0)),
            pl.BlockSpec(memory_space=pltpu.SMEM),
            pl.BlockSpec((None, 1, dv), lambda i, h: (h, 0, 0)),
        ],
        out_specs=pl.BlockSpec((None, s, dv), lambda i, h: (i, 0, h)),
        out_shape=jax.ShapeDtypeStruct((b, s, n_heads * dv), BF16),
        scratch_shapes=[pltpu.VMEM((s, dk), F32), pltpu.VMEM((s, dk), F32), pltpu.VMEM((s, dv), F32),
                        pltpu.VMEM((dk, dv), F32), pltpu.VMEM((dk, dv), F32)],
        compiler_params=_cparams("parallel", "parallel"),
        name="retention",
    )(proj3, proj3, proj3, proj3, cbt, sbt, dmask, dec, cdec, gn_w.reshape(n_heads, 1, dv))


def _s5_kernel(u_ref, m_ref, e_ref, f_ref, a_ref, y_ref, kv, hp, *, n_chunks, bn):
    p = S5_STATE
    u = u_ref[...]
    kv[...] = jnp.dot(u, e_ref[...], preferred_element_type=F32)
    a_re = a_ref[0:1, :]
    a_im = a_ref[1:2, :]
    is_fwd = lax.broadcasted_iota(jnp.int32, (bn, 2 * p), 1) < p
    h_re = jnp.zeros((bn, 2 * p), F32)
    h_im = jnp.zeros((bn, 2 * p), F32)
    for n in range(n_chunks):
        rf = pl.ds(n * bn, bn)
        rb = pl.ds((n_chunks - 1 - n) * bn, bn)
        hp[rf, 0:p] = h_re[:, 0:p]
        hp[rb, p:2 * p] = h_re[:, p:2 * p]
        hp[rf, 2 * p:3 * p] = h_im[:, 0:p]
        hp[rb, 3 * p:4 * p] = h_im[:, p:2 * p]
        kv_re = jnp.where(is_fwd, kv[rf, 0:2 * p], kv[rb, 0:2 * p])
        kv_im = jnp.where(is_fwd, kv[rf, 2 * p:4 * p], kv[rb, 2 * p:4 * p])
        h_re, h_im = (a_re * h_re - a_im * h_im + kv_re, a_re * h_im + a_im * h_re + kv_im)
    y = jnp.dot(u, m_ref[...], preferred_element_type=F32)
    y_ref[...] = y + jnp.dot(hp[...].astype(BF16), f_ref[...], preferred_element_type=F32)


def _s5_tables(lam_re, lam_im, log_dt, b_re, b_im, c_re, c_im):
    hp = lax.Precision.HIGHEST
    L = S5_CHUNK
    lam_re = lam_re.astype(F32)
    lam_im = lam_im.astype(F32)
    dt = jnp.exp(log_dt.astype(F32))[..., None]
    mag = jnp.exp(lam_re * dt)
    ar = mag * jnp.cos(lam_im * dt)
    ai = mag * jnp.sin(lam_im * dt)
    den = lam_re * lam_re + lam_im * lam_im
    nr = ar - 1.0
    cr = (nr * lam_re + ai * lam_im) / den
    ci = (ai * lam_re - nr * lam_im) / den
    b_re = b_re.astype(F32)
    b_im = b_im.astype(F32)
    bbr = cr[..., None] * b_re - ci[..., None] * b_im
    bbi = cr[..., None] * b_im + ci[..., None] * b_re
    c_re = c_re.astype(F32)
    c_im = c_im.astype(F32)
    tau = jnp.arange(L + 1, dtype=F32)[:, None, None, None]
    pmag = jnp.exp(tau * (lam_re * dt)[None])
    pang = tau * (lam_im * dt)[None]
    pr = pmag * jnp.cos(pang)
    pi = pmag * jnp.sin(pang)
    car = c_re[None] * pr[:L, :, :, None, :] - c_im[None] * pi[:L, :, :, None, :]
    cai = c_re[None] * pi[:L, :, :, None, :] + c_im[None] * pr[:L, :, :, None, :]
    kk = (jnp.einsum('tdgop,dgpc->tdgoc', car, bbr, precision=hp)
          - jnp.einsum('tdgop,dgpc->tdgoc', cai, bbi, precision=hp))
    t = jnp.arange(L)[:, None]
    j = jnp.arange(L)[None, :]
    kf = kk[jnp.clip(t - j, 0, L - 1), 0] * (t >= j)[:, :, None, None, None]
    kb = kk[jnp.clip(j - t, 0, L - 1), 1] * (j >= t)[:, :, None, None, None]
    g = lam_re.shape[1]
    c = b_re.shape[-1]
    m_t = (kf + kb).transpose(2, 1, 4, 0, 3).reshape(g, L * c, L * c)
    pf_r, pf_i = pr[:L, 0][::-1], pi[:L, 0][::-1]
    pb_r, pb_i = pr[:L, 1], pi[:L, 1]
    def bmul(xr, xi, d):
        re = xr[..., None] * bbr[d][None] - xi[..., None] * bbi[d][None]
        im = xr[..., None] * bbi[d][None] + xi[..., None] * bbr[d][None]
        return re.transpose(1, 0, 3, 2), im.transpose(1, 0, 3, 2)
    ef_r, ef_i = bmul(pf_r, pf_i, 0)
    eb_r, eb_i = bmul(pb_r, pb_i, 1)
    e_m = jnp.concatenate([ef_r, eb_r, ef_i, eb_i], axis=-1).reshape(g, L * c, 4 * S5_STATE)
    def cmul(xr, xi, d):
        re = c_re[d][None] * xr[:, :, None, :] - c_im[d][None] * xi[:, :, None, :]
        im = c_re[d][None] * xi[:, :, None, :] + c_im[d][None] * xr[:, :, None, :]
        return re.transpose(1, 3, 0, 2), -im.transpose(1, 3, 0, 2)
    ff_r, ff_i = cmul(pr[1:L + 1, 0], pi[1:L + 1, 0], 0)
    fb_r, fb_i = cmul(pr[1:L + 1, 1][::-1], pi[1:L + 1, 1][::-1], 1)
    f_m = jnp.concatenate([ff_r, fb_r, ff_i, fb_i], axis=1).reshape(g, 4 * S5_STATE, L * c)
    a_l = jnp.stack([jnp.concatenate([pr[L, 0], pr[L, 1]], axis=-1),
                     jnp.concatenate([pi[L, 0], pi[L, 1]], axis=-1)], axis=1)
    return m_t.astype(BF16), e_m.astype(BF16), f_m.astype(BF16), a_l


def _s5(uc3, tables):
    m_t, e_m, f_m, a_l = tables
    b, s, w = uc3.shape
    c = S5_GROUP
    g = w // c
    L = S5_CHUNK
    nc = s // L
    rows = nc * b
    ug = uc3.reshape(b, nc, L, g, c).transpose(3, 1, 0, 2, 4).reshape(g, rows, L * c).astype(BF16)
    ys = pl.pallas_call(
        functools.partial(_s5_kernel, n_chunks=nc, bn=b),
        grid=(g,),
        in_specs=[
            pl.BlockSpec((None, rows, L * c), lambda i: (i, 0, 0)),
            pl.BlockSpec((None, L * c, L * c), lambda i: (i, 0, 0)),
            pl.BlockSpec((None, L * c, 4 * S5_STATE), lambda i: (i, 0, 0)),
            pl.BlockSpec((None, 4 * S5_STATE, L * c), lambda i: (i, 0, 0)),
            pl.BlockSpec((None, 2, 2 * S5_STATE), lambda i: (i, 0, 0)),
        ],
        out_specs=pl.BlockSpec((None, rows, L * c), lambda i: (i, 0, 0)),
        out_shape=jax.ShapeDtypeStruct((g, rows, L * c), F32),
        scratch_shapes=[pltpu.VMEM((rows, 4 * S5_STATE), F32), pltpu.VMEM((rows, 4 * S5_STATE), F32)],
        compiler_params=_cparams("parallel"),
        name="s5",
    )(ug, m_t, e_m, f_m, a_l)
    return ys.reshape(g, nc, b, L, c).transpose(2, 1, 3, 0, 4).reshape(b * s, w)


def _glu_kernel(ys_ref, u_ref, d_ref, w_ref, o_ref):
    y = ys_ref[...] + u_ref[...] * d_ref[...]
    y = 0.5 * y * (1.0 + jnp.tanh(math.sqrt(2.0 / math.pi) * (y + 0.044715 * (y * y * y))))
    z = jnp.dot(y.astype(BF16), w_ref[...], preferred_element_type=F32)
    o_ref[...] = (y * _sigmoid(z)).astype(o_ref.dtype)


def _s5_glu(ys, proj2, u_off, d_w, glu_w, tm=512):
    t, w = ys.shape
    ub = u_off // w
    return pl.pallas_call(
        _glu_kernel,
        grid=(t // tm,),
        in_specs=[
            pl.BlockSpec((tm, w), lambda i: (i, 0)),
            pl.BlockSpec((tm, w), lambda i: (i, ub)),
            pl.BlockSpec((1, w), lambda i: (0, 0)),
            pl.BlockSpec((w, w), lambda i: (0, 0)),
        ],
        out_specs=pl.BlockSpec((tm, w), lambda i: (i, 0)),
        out_shape=jax.ShapeDtypeStruct((t, w), BF16),
        compiler_params=_cparams("parallel"),
        name="s5_glu",
    )(ys, proj2, d_w.reshape(1, w), glu_w)


def _cross_kernel(q_ref, kv_ref, o_ref, *, d_model, n_heads):
    hd = d_model // n_heads
    scale = hd ** -0.5
    for h in range(n_heads):
        q = q_ref[:, h * hd:(h + 1) * hd]
        k = kv_ref[:, h * hd:(h + 1) * hd]
        v = kv_ref[:, d_model + h * hd:d_model + (h + 1) * hd]
        sc = lax.dot_general(q, k, (((1,), (1,)), ((), ())), preferred_element_type=F32) * scale
        m = jnp.max(sc, axis=-1, keepdims=True)
        p = jnp.exp(sc - m)
        l = jnp.sum(p, axis=-1, keepdims=True)
        o = jnp.dot(p.astype(BF16), v, preferred_element_type=F32) / l
        o_ref[:, h * hd:(h + 1) * hd] = o.astype(o_ref.dtype)


def _cross_core(q3, kv3, tq=512):
    b, s, d = q3.shape
    nm = kv3.shape[1]
    return pl.pallas_call(
        functools.partial(_cross_kernel, d_model=d, n_heads=N_HEADS_X),
        grid=(b, s // tq),
        in_specs=[
            pl.BlockSpec((None, tq, d), lambda i, j: (i, j, 0)),
            pl.BlockSpec((None, nm, 2 * d), lambda i, j: (i, 0, 0)),
        ],
        out_specs=pl.BlockSpec((None, tq, d), lambda i, j: (i, j, 0)),
        out_shape=jax.ShapeDtypeStruct((b, s, d), BF16),
        compiler_params=_cparams("parallel", "parallel"),
        name="cross_attn",
    )(q3, kv3)


def _prefix_sum_lanes(x):
    n = x.shape[-1]
    lane = lax.broadcasted_iota(jnp.int32, x.shape, x.ndim - 1)
    k = 1
    while k < n:
        x = x + jnp.where(lane >= k, pltpu.roll(x, k, x.ndim - 1), 0.0)
        k *= 2
    return x


def _topk_kernel(lg_ref, idx_ref, gate_ref, *, capacity):
    lg = lg_ref[...]
    ne, s = lg.shape
    mx = jnp.max(lg, axis=0, keepdims=True)
    ex = jnp.exp(lg - mx)
    aff = ex / jnp.sum(ex, axis=0, keepdims=True)
    bits = pltpu.bitcast(aff, jnp.int32)

    def bit_body(i, thr):
        cand = thr | jnp.left_shift(jnp.int32(1), 30 - i)
        cnt = jnp.sum((bits >= cand).astype(F32), axis=1, keepdims=True)
        return jnp.where(cnt >= capacity, cand, thr)

    thr = lax.fori_loop(0, 31, bit_body, jnp.zeros((ne, 1), jnp.int32))
    gt = bits > thr
    eq = bits == thr
    n_gt = jnp.sum(gt.astype(F32), axis=1, keepdims=True)
    eq_rank = _prefix_sum_lanes(eq.astype(F32))
    sel = gt | (eq & (eq_rank <= capacity - n_gt))
    slot = _prefix_sum_lanes(sel.astype(F32)) - 1.0
    key = jnp.where(sel, slot, -1.0)
    tok = lax.broadcasted_iota(jnp.int32, (1, s), 1).astype(F32)
    lc = min(1024, s)
    for e in range(ne):
        idx_acc = jnp.zeros((capacity, 1), F32)
        gate_acc = jnp.zeros((capacity, 1), F32)
        for c0 in range(0, s, lc):
            slots = lax.broadcasted_iota(jnp.int32, (capacity, lc), 0).astype(F32)
            hit = key[e:e + 1, c0:c0 + lc] == slots
            idx_acc += jnp.sum(jnp.where(hit, tok[:, c0:c0 + lc], 0.0), axis=1, keepdims=True)
            gate_acc += jnp.sum(jnp.where(hit, aff[e:e + 1, c0:c0 + lc], 0.0), axis=1, keepdims=True)
        idx_ref[e] = idx_acc.astype(jnp.int32)
        gate_ref[e] = gate_acc


def _topk(logits, bn, capacity):
    ne, t = logits.shape
    s = t // bn
    idx, gate = pl.pallas_call(
        functools.partial(_topk_kernel, capacity=capacity),
        grid=(bn,),
        in_specs=[pl.BlockSpec((ne, s), lambda i: (0, i))],
        out_specs=[pl.BlockSpec((None, ne, capacity, 1), lambda i: (i, 0, 0, 0)),
                   pl.BlockSpec((None, ne, capacity, 1), lambda i: (i, 0, 0, 0))],
        out_shape=[jax.ShapeDtypeStruct((bn, ne, capacity, 1), jnp.int32),
                   jax.ShapeDtypeStruct((bn, ne, capacity, 1), F32)],
        compiler_params=_cparams("parallel"),
        name="topk",
    )(logits)
    return idx.reshape(bn, ne, capacity), gate


def _row_copy(src_hbm, dst, src_row, dst_row, sem):
    return pltpu.make_async_copy(src_hbm.at[pl.ds(src_row, 1), :], dst.at[pl.ds(dst_row, 1), :], sem)


def _ffn1_kernel(idx_ref, x_hbm, nw_ref, wg_ref, wu_ref, act_ref, xg, sem, *, seq):
    cap = xg.shape[0]
    base = pl.program_id(1) * seq

    def issue(c, carry):
        _row_copy(x_hbm, xg, base + idx_ref[0, c], c, sem).start()
        return carry

    def drain(c, carry):
        _row_copy(x_hbm, xg, base + idx_ref[0, c], c, sem).wait()
        return carry

    lax.fori_loop(0, cap, issue, 0)
    lax.fori_loop(0, cap, drain, 0)
    h = _rmsnorm_rows(xg[...], nw_ref[...]).astype(BF16)
    g = jnp.dot(h, wg_ref[...], preferred_element_type=F32)
    u = jnp.dot(h, wu_ref[...], preferred_element_type=F32)
    act_ref[...] = ((g * _sigmoid(g)) * u).astype(act_ref.dtype)


def _ffn1(idx, x2, nw, wg, wu, seq):
    bn, ne, cap = idx.shape
    d = x2.shape[1]
    ff = wg.shape[-1]
    return pl.pallas_call(
        functools.partial(_ffn1_kernel, seq=seq),
        grid=(ne, bn),
        in_specs=[
            pl.BlockSpec((None, None, 1, cap), lambda e, b: (b, e, 0, 0), memory_space=pltpu.SMEM),
            pl.BlockSpec(memory_space=pl.ANY),
            pl.BlockSpec((1, d), lambda e, b: (0, 0)),
            pl.BlockSpec((None, d, ff), lambda e, b: (e, 0, 0)),
            pl.BlockSpec((None, d, ff), lambda e, b: (e, 0, 0)),
        ],
        out_specs=pl.BlockSpec((None, None, cap, ff), lambda e, b: (e, b, 0, 0)),
        out_shape=jax.ShapeDtypeStruct((ne, bn, cap, ff), BF16),
        scratch_shapes=[pltpu.VMEM((cap, d), F32), pltpu.SemaphoreType.DMA(())],
        compiler_params=_cparams("arbitrary", "arbitrary"),
        name="moe_ffn1",
    )(idx.reshape(bn, ne, 1, cap), x2, nw.reshape(1, d), wg, wu)


def _ffn2_kernel(idx_ref, gate_ref, act_ref, wd_ref, x_in, x_out, xo, sem_in, sem_out, *, seq):
    del x_in
    cap = xo.shape[0]
    base = pl.program_id(1) * seq

    def issue_in(c, carry):
        _row_copy(x_out, xo, base + idx_ref[0, c], c, sem_in).start()
        return carry

    def drain_in(c, carry):
        _row_copy(x_out, xo, base + idx_ref[0, c], c, sem_in).wait()
        return carry

    def put_row(c):
        row = base + idx_ref[0, c]
        return pltpu.make_async_copy(xo.at[pl.ds(c, 1), :], x_out.at[pl.ds(row, 1), :], sem_out)

    def issue_out(c, carry):
        put_row(c).start()
        return carry

    def drain_out(c, carry):
        put_row(c).wait()
        return carry

    lax.fori_loop(0, cap, issue_in, 0)
    ye = jnp.dot(act_ref[...], wd_ref[...], preferred_element_type=F32) * gate_ref[...]
    lax.fori_loop(0, cap, drain_in, 0)
    xo[...] = xo[...] + ye
    lax.fori_loop(0, cap, issue_out, 0)
    lax.fori_loop(0, cap, drain_out, 0)


def _ffn2(idx, gate, act, wd, x2, seq):
    bn, ne, cap = idx.shape
    t, d = x2.shape
    ff = wd.shape[1]
    return pl.pallas_call(
        functools.partial(_ffn2_kernel, seq=seq),
        grid=(ne, bn),
        in_specs=[
            pl.BlockSpec((None, None, 1, cap), lambda e, b: (b, e, 0, 0), memory_space=pltpu.SMEM),
            pl.BlockSpec((None, None, cap, 1), lambda e, b: (b, e, 0, 0)),
            pl.BlockSpec((None, None, cap, ff), lambda e, b: (e, b, 0, 0)),
            pl.BlockSpec((None, ff, d), lambda e, b: (e, 0, 0)),
            pl.BlockSpec(memory_space=pl.ANY),
        ],
        out_specs=pl.BlockSpec(memory_space=pl.ANY),
        out_shape=jax.ShapeDtypeStruct((t, d), F32),
        input_output_aliases={4: 0},
        scratch_shapes=[pltpu.VMEM((cap, d), F32), pltpu.SemaphoreType.DMA(()), pltpu.SemaphoreType.DMA(())],
        compiler_params=_cparams("arbitrary", "arbitrary"),
        name="moe_ffn2",
    )(idx.reshape(bn, ne, 1, cap), gate, act, wd, x2)


def kernel(x, mem, positions, w_in, w_out, norm_mix_w, norm_cross_w, norm_mem_w, norm_ffn_w, final_norm_w, ret_decay, ret_gn_w, s5_lam_re, s5_lam_im, s5_log_dt, s5_b_re, s5_b_im, s5_c_re, s5_c_im, s5_d, s5_glu_w, cross_wq, cross_wkv, cross_wo, router_w, expert_w_gate, expert_w_up, expert_w_down):
    bn, s, d = x.shape
    depth = w_in.shape[0]
    n_mem = mem.shape[1]
    width_a = 3 * d // 8
    width_b = 3 * d // 8
    width_c = d - width_a - width_b
    n_heads_a = width_a // HEAD_DIM_A
    q_off_b = 3 * width_a
    u_off = q_off_b + 2 * N_HEADS_B * LANES + 2 * width_b
    capacity = EC_CAPACITY_FACTOR * s // N_EXPERTS
    t = bn * s

    ca, sa, cbt, sbt = _rope_tables(positions)
    x2 = x.reshape(t, d)
    mem2 = mem.reshape(bn * n_mem, d)
    for l in range(depth):
        h = _rmsnorm(x2, norm_mix_w[l], BF16)
        proj = _matmul(h, w_in[l].astype(BF16), F32)
        proj3 = proj.reshape(bn, s, -1)
        a_out = _attn_a(proj3, ca, sa, n_heads_a)
        b_out = _retention(proj3, cbt, sbt, ret_decay[l], ret_gn_w[l], q_off_b, N_HEADS_B)
        s5_tabs = _s5_tables(s5_lam_re[l], s5_lam_im[l], s5_log_dt[l], s5_b_re[l], s5_b_im[l],
                             s5_c_re[l], s5_c_im[l])
        ys = _s5(proj3[:, :, u_off:], s5_tabs)
        c_out = _s5_glu(ys, proj, u_off, s5_d[l], s5_glu_w[l].astype(BF16))
        x2 = _out_proj(a_out.reshape(t, width_a), b_out.reshape(t, width_b), c_out, w_out[l].astype(BF16), x2)

        h = _rmsnorm(x2, norm_cross_w[l], BF16)
        mn = _rmsnorm(mem2, norm_mem_w[l], BF16)
        q = _matmul(h, cross_wq[l].astype(BF16), BF16)
        kv = _matmul(mn, cross_wkv[l].astype(BF16), BF16)
        o = _cross_core(q.reshape(bn, s, d), kv.reshape(bn, n_mem, 2 * d))
        x2 = _matmul(o.reshape(t, d), cross_wo[l].astype(BF16), F32, res=x2)

        logits = _router_logits(x2, norm_ffn_w[l], router_w[l].T.astype(BF16))
        idx, gate = _topk(logits, bn, capacity)
        act = _ffn1(idx, x2, norm_ffn_w[l], expert_w_gate[l].astype(BF16), expert_w_up[l].astype(BF16), s)
        x2 = _ffn2(idx, gate, act, expert_w_down[l].astype(BF16), x2, s)
    return _rmsnorm(x2, final_norm_w, x.dtype).reshape(bn, s, d)
```

```python
import functools
import math

import numpy as np
import jax
import jax.numpy as jnp
from jax import lax
from jax.experimental import pallas as pl
from jax.experimental.pallas import tpu as pltpu

F32 = jnp.float32
BF16 = jnp.bfloat16

EPS = 1e-6
NEG_INF = -1e30
VMEM_LIMIT_BYTES = 56 * 1024 * 1024
LANES = 128

HEAD_DIM_A = 128
ROT_DIM_A = HEAD_DIM_A // 4
ROPE_THETA = 500000.0
DILATION_PAIRS = ((128, 1), (512, 4), (2048, 16))
ATT_BLK = 128
N_HEADS_B = 6
RET_CHUNK = 128
RET_THETA = 10000.0
S5_GROUP = 16
S5_STATE = 64
S5_CHUNK = 32
N_HEADS_X = 4
N_EXPERTS = 16
EC_CAPACITY_FACTOR = 2


def _cparams(*sem):
    return pltpu.CompilerParams(dimension_semantics=sem, vmem_limit_bytes=VMEM_LIMIT_BYTES)


def _sigmoid(x):
    return 1.0 / (1.0 + jnp.exp(-x))


def _rmsnorm_rows(x, w):
    ms = jnp.mean(x * x, axis=-1, keepdims=True)
    return x * lax.rsqrt(ms + EPS) * w


def _rmsnorm_kernel(x_ref, w_ref, o_ref):
    o_ref[...] = _rmsnorm_rows(x_ref[...], w_ref[...]).astype(o_ref.dtype)


def _rmsnorm(x2, w, out_dtype, tm=512):
    t, d = x2.shape
    tm = min(tm, t)
    return pl.pallas_call(
        _rmsnorm_kernel,
        grid=(t // tm,),
        in_specs=[pl.BlockSpec((tm, d), lambda i: (i, 0)), pl.BlockSpec((1, d), lambda i: (0, 0))],
        out_specs=pl.BlockSpec((tm, d), lambda i: (i, 0)),
        out_shape=jax.ShapeDtypeStruct((t, d), out_dtype),
        compiler_params=_cparams("parallel"),
        name="rmsnorm",
    )(x2, w.reshape(1, d))


def _router_kernel(x_ref, w_ref, rw_ref, lg_ref):
    h = _rmsnorm_rows(x_ref[...], w_ref[...]).astype(BF16)
    lg_ref[...] = lax.dot_general(rw_ref[...], h, (((1,), (1,)), ((), ())), preferred_element_type=F32)


def _router_logits(x2, w, rw_t, tm=512):
    t, d = x2.shape
    ne = rw_t.shape[0]
    return pl.pallas_call(
        _router_kernel,
        grid=(t // tm,),
        in_specs=[
            pl.BlockSpec((tm, d), lambda i: (i, 0)),
            pl.BlockSpec((1, d), lambda i: (0, 0)),
            pl.BlockSpec((ne, d), lambda i: (0, 0)),
        ],
        out_specs=pl.BlockSpec((ne, tm), lambda i: (0, i)),
        out_shape=jax.ShapeDtypeStruct((ne, t), F32),
        compiler_params=_cparams("parallel"),
        name="router_logits",
    )(x2, w.reshape(1, d), rw_t)


def _mm_kernel(a_ref, w_ref, o_ref):
    o_ref[...] = jnp.dot(a_ref[...], w_ref[...], preferred_element_type=F32).astype(o_ref.dtype)


def _mm_res_kernel(a_ref, w_ref, r_ref, o_ref):
    o_ref[...] = r_ref[...] + jnp.dot(a_ref[...], w_ref[...], preferred_element_type=F32)


def _mm3_res_kernel(a1_ref, a2_ref, a3_ref, w1_ref, w2_ref, w3_ref, r_ref, o_ref):
    acc = jnp.dot(a1_ref[...], w1_ref[...], preferred_element_type=F32)
    acc += jnp.dot(a2_ref[...], w2_ref[...], preferred_element_type=F32)
    acc += jnp.dot(a3_ref[...], w3_ref[...], preferred_element_type=F32)
    o_ref[...] = r_ref[...] + acc


def _matmul(a, w, out_dtype, res=None, tm=1024, tn=512):
    m, k = a.shape
    n = w.shape[1]
    tm = min(tm, m)
    tn = min(tn, n)
    in_specs = [pl.BlockSpec((tm, k), lambda i, j: (i, 0)), pl.BlockSpec((k, tn), lambda i, j: (0, j))]
    args = [a, w]
    kern = _mm_kernel
    if res is not None:
        in_specs.append(pl.BlockSpec((tm, tn), lambda i, j: (i, j)))
        args.append(res)
        kern = _mm_res_kernel
    return pl.pallas_call(
        kern,
        grid=(m // tm, n // tn),
        in_specs=in_specs,
        out_specs=pl.BlockSpec((tm, tn), lambda i, j: (i, j)),
        out_shape=jax.ShapeDtypeStruct((m, n), out_dtype),
        compiler_params=_cparams("parallel", "parallel"),
        name="matmul",
    )(*args)


def _out_proj(a1, a2, a3, w, res, tm=1024, tn=512):
    m = a1.shape[0]
    k1, k2, k3 = a1.shape[1], a2.shape[1], a3.shape[1]
    n = w.shape[1]
    assert k1 == k2 and (k1 + k2) % k3 == 0
    return pl.pallas_call(
        _mm3_res_kernel,
        grid=(m // tm, n // tn),
        in_specs=[
            pl.BlockSpec((tm, k1), lambda i, j: (i, 0)),
            pl.BlockSpec((tm, k2), lambda i, j: (i, 0)),
            pl.BlockSpec((tm, k3), lambda i, j: (i, 0)),
            pl.BlockSpec((k1, tn), lambda i, j: (0, j)),
            pl.BlockSpec((k2, tn), lambda i, j: (1, j)),
            pl.BlockSpec((k3, tn), lambda i, j: ((k1 + k2) // k3, j)),
            pl.BlockSpec((tm, tn), lambda i, j: (i, j)),
        ],
        out_specs=pl.BlockSpec((tm, tn), lambda i, j: (i, j)),
        out_shape=jax.ShapeDtypeStruct((m, n), F32),
        compiler_params=_cparams("parallel", "parallel"),
        name="out_proj",
    )(a1, a2, a3, w, w, w, res)


def _rope_kernel(pos_ref, fa_ref, fb_ref, ca_ref, sa_ref, cb_ref, sb_ref):
    pos = pos_ref[...].astype(F32)
    ang_a = pos * fa_ref[...]
    ca_ref[...] = jnp.cos(ang_a)
    sa_ref[...] = jnp.sin(ang_a)
    ang_b = pos * fb_ref[...]
    lane = lax.broadcasted_iota(jnp.int32, ang_b.shape, 1)
    sin_b = jnp.sin(ang_b)
    cb_ref[...] = jnp.cos(ang_b)
    sb_ref[...] = jnp.where(lane < LANES // 2, -sin_b, sin_b)


def _rope_tables(positions):
    b, s = positions.shape
    half_a = ROT_DIM_A // 2
    fa = ROPE_THETA ** (-(jnp.arange(half_a, dtype=F32) * 2.0 / ROT_DIM_A))
    fa_row = jnp.concatenate([fa, fa, jnp.zeros((LANES - 2 * half_a,), F32)]).reshape(1, LANES)
    fb = RET_THETA ** (-jnp.linspace(0.0, 1.0, LANES // 2, dtype=F32))
    fb_row = jnp.concatenate([fb, fb]).reshape(1, LANES)
    tab = jax.ShapeDtypeStruct((b, s, LANES), F32)
    row = pl.BlockSpec((1, LANES), lambda i: (0, 0))
    out = pl.BlockSpec((None, s, LANES), lambda i: (i, 0, 0))
    return pl.pallas_call(
        _rope_kernel,
        grid=(b,),
        in_specs=[pl.BlockSpec((None, s, 1), lambda i: (i, 0, 0)), row, row],
        out_specs=[out, out, out, out],
        out_shape=[tab, tab, tab, tab],
        compiler_params=_cparams("parallel"),
        name="rope_tables",
    )(positions.reshape(b, s, 1), fa_row, fb_row)


def _attn_a_kernel(q_ref, k_ref, v_ref, c_ref, s_ref, o_ref, qn, kn, accn, mn, ln, *, seq, radius):
    rc = 128
    scale = HEAD_DIM_A ** -0.5
    half = ROT_DIM_A // 2

    def rot_body(i, carry):
        rows = pl.ds(pl.multiple_of(i * rc, rc), rc)
        lane = lax.broadcasted_iota(jnp.int32, (rc, LANES), 1)
        c = c_ref[rows, :]
        s = s_ref[rows, :]
        s_lo = jnp.where(lane < half, -s, 0.0)
        s_hi = jnp.where(lane >= half, s, 0.0)
        q = q_ref[rows, :]
        k = k_ref[rows, :]
        qn[rows, :] = (q * c + pltpu.roll(q, LANES - half, 1) * s_lo + pltpu.roll(q, half, 1) * s_hi) * scale
        kn[rows, :] = k * c + pltpu.roll(k, LANES - half, 1) * s_lo + pltpu.roll(k, half, 1) * s_hi
        return carry

    lax.fori_loop(0, seq // rc, rot_body, 0)

    for branch, (_, dil) in enumerate(DILATION_PAIRS):
        sub = seq // dil
        blk = sub if sub <= 2 * ATT_BLK else ATT_BLK
        win = min(sub, blk + 2 * radius)
        nbr = sub // blk
        dmat = (lax.broadcasted_iota(jnp.int32, (blk, win), 1)
                - lax.broadcasted_iota(jnp.int32, (blk, win), 0))

        def rows_of(start, size, dil=dil):
            return pl.ds(start, size) if dil == 1 else pl.ds(start, size, stride=dil)

        def blk_body(n, carry, branch=branch, dil=dil, sub=sub, nbr=nbr, rows_of=rows_of,
                     blk=blk, win=win, dmat=dmat):
            r = n // nbr
            j = n - r * nbr
            wrel = jnp.clip(j * blk - radius, 0, sub - win)
            q_rows = rows_of(r + dil * (j * blk), blk)
            k_rows = rows_of(r + dil * wrel, win)
            qb = qn[q_rows, :].astype(BF16)
            kw = kn[k_rows, :].astype(BF16)
            vw = v_ref[k_rows, :].astype(BF16)
            sc = lax.dot_general(qb, kw, (((1,), (1,)), ((), ())), preferred_element_type=F32)
            sc = jnp.where(jnp.abs(dmat + (wrel - j * blk)) <= radius, sc, NEG_INF)
            m = jnp.max(sc, axis=1, keepdims=True)
            p = jnp.exp(sc - m)
            l = jnp.broadcast_to(jnp.sum(p, axis=1, keepdims=True), (blk, LANES))
            acc = jnp.dot(p.astype(BF16), vw, preferred_element_type=F32)
            m = jnp.broadcast_to(m, (blk, LANES))
            if branch > 0:
                m_old = mn[q_rows, :]
                m_max = jnp.maximum(m_old, m)
                e_old = jnp.exp(m_old - m_max)
                e_new = jnp.exp(m - m_max)
                acc = accn[q_rows, :] * e_old + acc * e_new
                l = ln[q_rows, :] * e_old + l * e_new
                m = m_max
            accn[q_rows, :] = acc
            mn[q_rows, :] = m
            ln[q_rows, :] = l
            return carry

        lax.fori_loop(0, seq // blk, blk_body, 0, unroll=4 if blk == ATT_BLK else 2)

    def out_body(i, carry):
        rows = pl.ds(pl.multiple_of(i * rc, rc), rc)
        o_ref[rows, :] = (accn[rows, :] / ln[rows, :]).astype(o_ref.dtype)
        return carry

    lax.fori_loop(0, seq // rc, out_body, 0)


def _attn_a(proj3, ca, sa, n_heads):
    b, s, _ = proj3.shape
    radii = {w // (2 * d) for w, d in DILATION_PAIRS}
    assert len(radii) == 1
    radius = radii.pop()
    for _, d in DILATION_PAIRS:
        assert s % (d * ATT_BLK) == 0 and s // d >= 2 * ATT_BLK

    def col(off):
        return pl.BlockSpec((None, s, HEAD_DIM_A), lambda i, h: (i, 0, off + h))

    tab = pl.BlockSpec((None, s, LANES), lambda i, h: (i, 0, 0))
    f32s = lambda: pltpu.VMEM((s, LANES), F32)
    return pl.pallas_call(
        functools.partial(_attn_a_kernel, seq=s, radius=radius),
        grid=(b, n_heads),
        in_specs=[col(0), col(n_heads), col(2 * n_heads), tab, tab],
        out_specs=pl.BlockSpec((None, s, HEAD_DIM_A), lambda i, h: (i, 0, h)),
        out_shape=jax.ShapeDtypeStruct((b, s, n_heads * HEAD_DIM_A), BF16),
        scratch_shapes=[f32s(), f32s(), f32s(), f32s(), f32s()],
        compiler_params=_cparams("parallel", "parallel"),
        name="attn_a",
    )(proj3, proj3, proj3, ca, sa)


def _ret_kernel(q_ref, k_ref, v_ref, g_ref, c_ref, s_ref, dm_ref, dec_ref, cdec_ref, gn_ref, o_ref,
                qn, kn, racc, sf, sb, *, seq):
    cb = RET_CHUNK
    nc = seq // cb
    dk = q_ref.shape[-1]
    dv = v_ref.shape[-1]
    h = pl.program_id(1)
    kscale = dk ** -0.5

    def rot_body(i, carry):
        rows = pl.ds(pl.multiple_of(i * cb, cb), cb)
        c = c_ref[rows, :]
        s = s_ref[rows, :]
        q = q_ref[rows, :]
        k = k_ref[rows, :]
        qn[rows, :] = q * c + pltpu.roll(q, dk // 2, 1) * s
        kn[rows, :] = (k * c + pltpu.roll(k, dk // 2, 1) * s) * kscale
        racc[rows, :] = jnp.zeros((cb, dv), F32)
        return carry

    lax.fori_loop(0, nc, rot_body, 0)
    sf[...] = jnp.zeros(sf.shape, F32)
    sb[...] = jnp.zeros(sb.shape, F32)
    c_f = cdec_ref[h, 0]
    c_b = cdec_ref[h, 1]

    def body(n, carry):
        rows_f = pl.ds(pl.multiple_of(n * cb, cb), cb)
        rows_b = pl.ds(pl.multiple_of((nc - 1 - n) * cb, cb), cb)
        q = qn[rows_f, :]
        k = kn[rows_f, :]
        v = v_ref[rows_f, :].astype(BF16)
        sc = lax.dot_general(q.astype(BF16), k.astype(BF16), (((1,), (1,)), ((), ())),
                             preferred_element_type=F32) * dm_ref[...]
        intra = jnp.dot(sc.astype(BF16), v, preferred_element_type=F32)
        cross = jnp.dot((q * dec_ref[0]).astype(BF16), sf[...].astype(BF16), preferred_element_type=F32)
        racc[rows_f, :] += intra + cross
        kd = (k * dec_ref[1]).T.astype(BF16)
        sf[...] = c_f * sf[...] + jnp.dot(kd, v, preferred_element_type=F32)
        q2 = qn[rows_b, :]
        k2 = kn[rows_b, :]
        v2 = v_ref[rows_b, :].astype(BF16)
        cross_b = jnp.dot((q2 * dec_ref[2]).astype(BF16), sb[...].astype(BF16), preferred_element_type=F32)
        racc[rows_b, :] += cross_b
        kd2 = (k2 * dec_ref[3]).T.astype(BF16)
        sb[...] = c_b * sb[...] + jnp.dot(kd2, v2, preferred_element_type=F32)
        return carry

    lax.fori_loop(0, nc, body, 0)

    def out_body(i, carry):
        rows = pl.ds(pl.multiple_of(i * cb, cb), cb)
        r = racc[rows, :]
        mu = jnp.mean(r, axis=-1, keepdims=True)
        xc = r - mu
        var = jnp.mean(xc * xc, axis=-1, keepdims=True)
        y = xc * lax.rsqrt(var + EPS) * gn_ref[...]
        g = g_ref[rows, :]
        o_ref[rows, :] = (y * (g * _sigmoid(g))).astype(o_ref.dtype)
        return carry

    lax.fori_loop(0, nc, out_body, 0)


def _ret_tables(ret_decay_l):
    cb = RET_CHUNK
    log_g = jax.nn.log_sigmoid(ret_decay_l.astype(F32))
    lf = log_g[0][:, None, None]
    lb = log_g[1][:, None, None]
    i = jnp.arange(cb)[:, None]
    j = jnp.arange(cb)[None, :]
    dfwd = jnp.where(i >= j, i - j, 0).astype(F32)[None]
    dbwd = jnp.where(j > i, j - i, 0).astype(F32)[None]
    dmask = jnp.where((i >= j)[None], jnp.exp(dfwd * lf), jnp.exp(dbwd * lb))
    pos = jnp.arange(cb, dtype=F32)[None, :]
    lf2, lb2 = log_g[0][:, None], log_g[1][:, None]
    dec = jnp.stack([jnp.exp((pos + 1.0) * lf2), jnp.exp((cb - 1.0 - pos) * lf2),
                     jnp.exp((cb - pos) * lb2), jnp.exp(pos * lb2)], axis=1)
    dec = jnp.broadcast_to(dec[..., None], dec.shape + (LANES,))
    cdec = jnp.exp(cb * log_g).T
    return dmask, dec, cdec


def _retention(proj3, cbt, sbt, ret_decay_l, gn_w, q_off, n_heads):
    b, s, _ = proj3.shape
    dk = LANES
    dv = 2 * dk
    dmask, dec, cdec = _ret_tables(ret_decay_l)
    qb0 = q_off // dk
    kb0 = qb0 + n_heads
    vb0 = (q_off + 2 * n_heads * dk) // dv
    gb0 = vb0 + n_heads
    tab = pl.BlockSpec((None, s, LANES), lambda i, h: (i, 0, 0))
    return pl.pallas_call(
        functools.partial(_ret_kernel, seq=s),
        grid=(b, n_heads),
        in_specs=[
            pl.BlockSpec((None, s, dk), lambda i, h: (i, 0, qb0 + h)),
            pl.BlockSpec((None, s, dk), lambda i, h: (i, 0, kb0 + h)),
            pl.BlockSpec((None, s, dv), lambda i, h: (i, 0, vb0 + h)),
            pl.BlockSpec((None, s, dv), lambda i, h: (i, 0, gb0 + h)),
            tab, tab,
            pl.BlockSpec((None, RET_CHUNK, RET_CHUNK), lambda i, h: (h, 0, 0)),
            pl.BlockSpec((None, 4, RET_CHUNK, LANES), lambda i, h: (h, 0, 0, 0)),
            pl.BlockSpec(memory_space=pltpu.SMEM),
            pl.BlockSpec((None, 1, dv), lambda i, h: (h, 0, 0)),
        ],
        out_specs=pl.BlockSpec((None, s, dv), lambda i, h: (i, 0, h)),
        out_shape=jax.ShapeDtypeStruct((b, s, n_heads * dv), BF16),
        scratch_shapes=[pltpu.VMEM((s, dk), F32), pltpu.VMEM((s, dk), F32), pltpu.VMEM((s, dv), F32),
                        pltpu.VMEM((dk, dv), F32), pltpu.VMEM((dk, dv), F32)],
        compiler_params=_cparams("parallel", "parallel"),
        name="retention",
    )(proj3, proj3, proj3, proj3, cbt, sbt, dmask, dec, cdec, gn_w.reshape(n_heads, 1, dv))


def _s5_kernel(u_ref, strip_ref, e_ref, f_ref, a_ref, y_ref, kv, hp, m_sc, *, n_chunks, bn):
    p = S5_STATE
    c = S5_GROUP
    L = S5_CHUNK
    strip = strip_ref[...]
    for j in range(L):
        m_sc[j * c:(j + 1) * c, :] = strip[:, (L - 1 - j) * c:(2 * L - 1 - j) * c].astype(BF16)
    u = u_ref[...]
    kv[...] = jnp.dot(u, e_ref[...], preferred_element_type=F32)
    a_re = a_ref[0:1, :]
    a_im = a_ref[1:2, :]
    is_fwd = lax.broadcasted_iota(jnp.int32, (bn, 2 * p), 1) < p
    h_re = jnp.zeros((bn, 2 * p), F32)
    h_im = jnp.zeros((bn, 2 * p), F32)
    for n in range(n_chunks):
        rf = pl.ds(n * bn, bn)
        rb = pl.ds((n_chunks - 1 - n) * bn, bn)
        hp[rf, 0:p] = h_re[:, 0:p]
        hp[rb, p:2 * p] = h_re[:, p:2 * p]
        hp[rf, 2 * p:3 * p] = h_im[:, 0:p]
        hp[rb, 3 * p:4 * p] = h_im[:, p:2 * p]
        kv_re = jnp.where(is_fwd, kv[rf, 0:2 * p], kv[rb, 0:2 * p])
        kv_im = jnp.where(is_fwd, kv[rf, 2 * p:4 * p], kv[rb, 2 * p:4 * p])
        h_re, h_im = (a_re * h_re - a_im * h_im + kv_re, a_re * h_im + a_im * h_re + kv_im)
    y = jnp.dot(u, m_sc[...], preferred_element_type=F32)
    y_ref[...] = y + jnp.dot(hp[...].astype(BF16), f_ref[...], preferred_element_type=F32)


def _s5_tables(lam_re, lam_im, log_dt, b_re, b_im, c_re, c_im):
    hp = lax.Precision.HIGHEST
    L = S5_CHUNK
    lam_re = lam_re.astype(F32)
    lam_im = lam_im.astype(F32)
    dt = jnp.exp(log_dt.astype(F32))[..., None]
    mag = jnp.exp(lam_re * dt)
    ar = mag * jnp.cos(lam_im * dt)
    ai = mag * jnp.sin(lam_im * dt)
    den = lam_re * lam_re + lam_im * lam_im
    nr = ar - 1.0
    cr = (nr * lam_re + ai * lam_im) / den
    ci = (ai * lam_re - nr * lam_im) / den
    b_re = b_re.astype(F32)
    b_im = b_im.astype(F32)
    bbr = cr[..., None] * b_re - ci[..., None] * b_im
    bbi = cr[..., None] * b_im + ci[..., None] * b_re
    c_re = c_re.astype(F32)
    c_im = c_im.astype(F32)
    tau = jnp.arange(L + 1, dtype=F32)[:, None, None, None]
    pmag = jnp.exp(tau * (lam_re * dt)[None])
    pang = tau * (lam_im * dt)[None]
    pr = pmag * jnp.cos(pang)
    pi = pmag * jnp.sin(pang)
    car = c_re[None] * pr[:L, :, :, None, :] - c_im[None] * pi[:L, :, :, None, :]
    cai = c_re[None] * pi[:L, :, :, None, :] + c_im[None] * pr[:L, :, :, None, :]
    kk = (jnp.einsum('tdgop,dgpc->tdgoc', car, bbr, precision=hp)
          - jnp.einsum('tdgop,dgpc->tdgoc', cai, bbi, precision=hp))
    g = lam_re.shape[1]
    c = b_re.shape[-1]
    lags = jnp.concatenate([kk[1:, 1][::-1], (kk[0, 0] + kk[0, 1])[None], kk[1:, 0]], axis=0)
    strip = lags.transpose(1, 3, 0, 2).reshape(g, c, (2 * L - 1) * c)
    strip = jnp.pad(strip, ((0, 0), (0, 0), (0, c)))
    pf_r, pf_i = pr[:L, 0][::-1], pi[:L, 0][::-1]
    pb_r, pb_i = pr[:L, 1], pi[:L, 1]
    def bmul(xr, xi, d):
        re = xr[..., None] * bbr[d][None] - xi[..., None] * bbi[d][None]
        im = xr[..., None] * bbi[d][None] + xi[..., None] * bbr[d][None]
        return re.transpose(1, 0, 3, 2), im.transpose(1, 0, 3, 2)
    ef_r, ef_i = bmul(pf_r, pf_i, 0)
    eb_r, eb_i = bmul(pb_r, pb_i, 1)
    e_m = jnp.concatenate([ef_r, eb_r, ef_i, eb_i], axis=-1).reshape(g, L * c, 4 * S5_STATE)
    def cmul(xr, xi, d):
        re = c_re[d][None] * xr[:, :, None, :] - c_im[d][None] * xi[:, :, None, :]
        im = c_re[d][None] * xi[:, :, None, :] + c_im[d][None] * xr[:, :, None, :]
        return re.transpose(1, 3, 0, 2), -im.transpose(1, 3, 0, 2)
    ff_r, ff_i = cmul(pr[1:L + 1, 0], pi[1:L + 1, 0], 0)
    fb_r, fb_i = cmul(pr[1:L + 1, 1][::-1], pi[1:L + 1, 1][::-1], 1)
    f_m = jnp.concatenate([ff_r, fb_r, ff_i, fb_i], axis=1).reshape(g, 4 * S5_STATE, L * c)
    a_l = jnp.stack([jnp.concatenate([pr[L, 0], pr[L, 1]], axis=-1),
                     jnp.concatenate([pi[L, 0], pi[L, 1]], axis=-1)], axis=1)
    return strip, e_m.astype(BF16), f_m.astype(BF16), a_l


def _s5(uc3, tables):
    strip, e_m, f_m, a_l = tables
    b, s, w = uc3.shape
    c = S5_GROUP
    g = w // c
    L = S5_CHUNK
    nc = s // L
    rows = nc * b
    ug = uc3.reshape(b, nc, L, g, c).transpose(3, 1, 0, 2, 4).reshape(g, rows, L * c).astype(BF16)
    ys = pl.pallas_call(
        functools.partial(_s5_kernel, n_chunks=nc, bn=b),
        grid=(g,),
        in_specs=[
            pl.BlockSpec((None, rows, L * c), lambda i: (i, 0, 0)),
            pl.BlockSpec((None, c, 2 * L * c), lambda i: (i, 0, 0)),
            pl.BlockSpec((None, L * c, 4 * S5_STATE), lambda i: (i, 0, 0)),
            pl.BlockSpec((None, 4 * S5_STATE, L * c), lambda i: (i, 0, 0)),
            pl.BlockSpec((None, 2, 2 * S5_STATE), lambda i: (i, 0, 0)),
        ],
        out_specs=pl.BlockSpec((None, rows, L * c), lambda i: (i, 0, 0)),
        out_shape=jax.ShapeDtypeStruct((g, rows, L * c), F32),
        scratch_shapes=[pltpu.VMEM((rows, 4 * S5_STATE), F32), pltpu.VMEM((rows, 4 * S5_STATE), F32),
                        pltpu.VMEM((L * c, L * c), BF16)],
        compiler_params=_cparams("parallel"),
        name="s5",
    )(ug, strip, e_m, f_m, a_l)
    return ys.reshape(g, nc, b, L, c).transpose(2, 1, 3, 0, 4).reshape(b * s, w)


def _glu_kernel(ys_ref, u_ref, d_ref, w_ref, o_ref):
    y = ys_ref[...] + u_ref[...] * d_ref[...]
    y = 0.5 * y * (1.0 + jnp.tanh(math.sqrt(2.0 / math.pi) * (y + 0.044715 * (y * y * y))))
    z = jnp.dot(y.astype(BF16), w_ref[...], preferred_element_type=F32)
    o_ref[...] = (y * _sigmoid(z)).astype(o_ref.dtype)


def _s5_glu(ys, proj2, u_off, d_w, glu_w, tm=512):
    t, w = ys.shape
    ub = u_off // w
    return pl.pallas_call(
        _glu_kernel,
        grid=(t // tm,),
        in_specs=[
            pl.BlockSpec((tm, w), lambda i: (i, 0)),
            pl.BlockSpec((tm, w), lambda i: (i, ub)),
            pl.BlockSpec((1, w), lambda i: (0, 0)),
            pl.BlockSpec((w, w), lambda i: (0, 0)),
        ],
        out_specs=pl.BlockSpec((tm, w), lambda i: (i, 0)),
        out_shape=jax.ShapeDtypeStruct((t, w), BF16),
        compiler_params=_cparams("parallel"),
        name="s5_glu",
    )(ys, proj2, d_w.reshape(1, w), glu_w)


def _cross_qk_kernel(wq_ref, km_ref, a_ref):
    a_ref[...] = lax.dot_general(wq_ref[...], km_ref[...], (((1,), (1,)), ((), ())),
                                 preferred_element_type=F32).astype(a_ref.dtype)


def _cross_vo_kernel(vm_ref, wo_ref, vw_ref):
    vw_ref[...] = jnp.dot(vm_ref[...], wo_ref[...], preferred_element_type=F32).astype(vw_ref.dtype)


def _cross_fold(kv3, wq, wo, n_heads):
    b, nm, d2 = kv3.shape
    d = d2 // 2
    hd = d // n_heads
    a = pl.pallas_call(
        _cross_qk_kernel,
        grid=(b, n_heads),
        in_specs=[pl.BlockSpec((d, hd), lambda i, h: (0, h)),
                  pl.BlockSpec((None, nm, hd), lambda i, h: (i, 0, h))],
        out_specs=pl.BlockSpec((None, d, nm), lambda i, h: (i, 0, h)),
        out_shape=jax.ShapeDtypeStruct((b, d, n_heads * nm), BF16),
        compiler_params=_cparams("parallel", "parallel"),
        name="cross_fold_qk",
    )(wq, kv3)
    vw = pl.pallas_call(
        _cross_vo_kernel,
        grid=(b, n_heads),
        in_specs=[pl.BlockSpec((None, nm, hd), lambda i, h: (i, 0, n_heads + h)),
                  pl.BlockSpec((hd, d), lambda i, h: (h, 0))],
        out_specs=pl.BlockSpec((None, nm, d), lambda i, h: (i, h, 0)),
        out_shape=jax.ShapeDtypeStruct((b, n_heads * nm, d), BF16),
        compiler_params=_cparams("parallel", "parallel"),
        name="cross_fold_vo",
    )(kv3, wo)
    return a, vw


def _cross_probs_kernel(x_ref, nw_ref, a_ref, p_ref, *, n_heads, scale):
    h = _rmsnorm_rows(x_ref[...], nw_ref[...]).astype(BF16)
    sc = jnp.dot(h, a_ref[...], preferred_element_type=F32) * scale
    nm = sc.shape[1] // n_heads
    for hh in range(n_heads):
        s_h = sc[:, hh * nm:(hh + 1) * nm]
        m = jnp.max(s_h, axis=-1, keepdims=True)
        e = jnp.exp(s_h - m)
        p_ref[:, hh * nm:(hh + 1) * nm] = (e / jnp.sum(e, axis=-1, keepdims=True)).astype(p_ref.dtype)


def _cross_probs(x3, nw, a, n_heads, tq=512):
    b, s, d = x3.shape
    hm = a.shape[2]
    scale = (d // n_heads) ** -0.5
    return pl.pallas_call(
        functools.partial(_cross_probs_kernel, n_heads=n_heads, scale=scale),
        grid=(b, s // tq),
        in_specs=[pl.BlockSpec((None, tq, d), lambda i, j: (i, j, 0)),
                  pl.BlockSpec((1, d), lambda i, j: (0, 0)),
                  pl.BlockSpec((None, d, hm), lambda i, j: (i, 0, 0))],
        out_specs=pl.BlockSpec((None, tq, hm), lambda i, j: (i, j, 0)),
        out_shape=jax.ShapeDtypeStruct((b, s, hm), BF16),
        compiler_params=_cparams("parallel", "parallel"),
        name="cross_probs",
    )(x3, nw.reshape(1, d), a)


def _bmm_res_kernel(a_ref, w_ref, r_ref, o_ref):
    o_ref[...] = r_ref[...] + jnp.dot(a_ref[...], w_ref[...], preferred_element_type=F32)


def _bmm_res(a3, w3, res3, tm=1024, tn=1024):
    b, s, k = a3.shape
    n = w3.shape[2]
    return pl.pallas_call(
        _bmm_res_kernel,
        grid=(b, s // tm, n // tn),
        in_specs=[pl.BlockSpec((None, tm, k), lambda i, j, c: (i, j, 0)),
                  pl.BlockSpec((None, k, tn), lambda i, j, c: (i, 0, c)),
                  pl.BlockSpec((None, tm, tn), lambda i, j, c: (i, j, c))],
        out_specs=pl.BlockSpec((None, tm, tn), lambda i, j, c: (i, j, c)),
        out_shape=jax.ShapeDtypeStruct((b, s, n), F32),
        compiler_params=_cparams("parallel", "parallel", "parallel"),
        name="cross_out",
    )(a3, w3, res3)


def _prefix_sum_lanes(x):
    n = x.shape[-1]
    lane = lax.broadcasted_iota(jnp.int32, x.shape, x.ndim - 1)
    k = 1
    while k < n:
        x = x + jnp.where(lane >= k, pltpu.roll(x, k, x.ndim - 1), 0.0)
        k *= 2
    return x


def _topk_kernel(lg_ref, idx_ref, gate_ref, *, capacity):
    lg = lg_ref[...]
    ne, s = lg.shape
    mx = jnp.max(lg, axis=0, keepdims=True)
    ex = jnp.exp(lg - mx)
    aff = ex / jnp.sum(ex, axis=0, keepdims=True)
    bits = pltpu.bitcast(aff, jnp.int32)

    def bit_body(i, thr):
        cand = thr | jnp.left_shift(jnp.int32(1), 30 - i)
        cnt = jnp.sum((bits >= cand).astype(F32), axis=1, keepdims=True)
        return jnp.where(cnt >= capacity, cand, thr)

    thr = lax.fori_loop(0, 31, bit_body, jnp.zeros((ne, 1), jnp.int32))
    gt = bits > thr
    eq = bits == thr
    n_gt = jnp.sum(gt.astype(F32), axis=1, keepdims=True)
    eq_rank = _prefix_sum_lanes(eq.astype(F32))
    sel = gt | (eq & (eq_rank <= capacity - n_gt))
    slot = _prefix_sum_lanes(sel.astype(F32)) - 1.0
    key = jnp.where(sel, slot, -1.0)
    tok = lax.broadcasted_iota(jnp.int32, (1, s), 1).astype(F32)
    lc = min(1024, s)
    for e in range(ne):
        idx_acc = jnp.zeros((capacity, 1), F32)
        gate_acc = jnp.zeros((capacity, 1), F32)
        for c0 in range(0, s, lc):
            slots = lax.broadcasted_iota(jnp.int32, (capacity, lc), 0).astype(F32)
            hit = key[e:e + 1, c0:c0 + lc] == slots
            idx_acc += jnp.sum(jnp.where(hit, tok[:, c0:c0 + lc], 0.0), axis=1, keepdims=True)
            gate_acc += jnp.sum(jnp.where(hit, aff[e:e + 1, c0:c0 + lc], 0.0), axis=1, keepdims=True)
        idx_ref[e] = idx_acc.astype(jnp.int32)
        gate_ref[e] = gate_acc


def _topk(logits, bn, capacity):
    ne, t = logits.shape
    s = t // bn
    idx, gate = pl.pallas_call(
        functools.partial(_topk_kernel, capacity=capacity),
        grid=(bn,),
        in_specs=[pl.BlockSpec((ne, s), lambda i: (0, i))],
        out_specs=[pl.BlockSpec((None, ne, capacity, 1), lambda i: (i, 0, 0, 0)),
                   pl.BlockSpec((None, ne, capacity, 1), lambda i: (i, 0, 0, 0))],
        out_shape=[jax.ShapeDtypeStruct((bn, ne, capacity, 1), jnp.int32),
                   jax.ShapeDtypeStruct((bn, ne, capacity, 1), F32)],
        compiler_params=_cparams("parallel"),
        name="topk",
    )(logits)
    return idx.reshape(bn, ne, capacity), gate


def _row_copy(src, dst, src_row, dst_row, sem):
    return pltpu.make_async_copy(src.at[pl.ds(src_row, 1), :], dst.at[pl.ds(dst_row, 1), :], sem)


DMA_LOOP_UNROLL = 8


def _near_step(e, b, n_e, n_b, delta):
    lin = jnp.clip(e * n_b + b + delta, 0, n_e * n_b - 1)
    return lin // n_b, lin % n_b


def _idx_spec(n_e, n_b, cap, delta):
    def index_map(e, b):
        e2, b2 = _near_step(e, b, n_e, n_b, delta)
        return (b2, e2, 0, 0)

    return pl.BlockSpec((None, None, 1, cap), index_map, memory_space=pltpu.SMEM)


def _ffn1_kernel(idx_ref, idx_next_ref, x_hbm, nw_ref, wg_ref, wu_ref, act_ref, xg, sem, *, seq):
    cap = xg.shape[0]
    e, b = pl.program_id(0), pl.program_id(1)
    n_e, n_b = pl.num_programs(0), pl.num_programs(1)
    step = e * n_b + b

    def gather(ir, base):
        def issue(c, carry):
            _row_copy(x_hbm, xg, base + ir[0, c], c, sem).start()
            return carry

        def drain(c, carry):
            _row_copy(x_hbm, xg, base + ir[0, c], c, sem).wait()
            return carry

        return issue, drain

    issue_cur, drain_cur = gather(idx_ref, b * seq)

    @pl.when(step == 0)
    def _():
        lax.fori_loop(0, cap, issue_cur, 0, unroll=DMA_LOOP_UNROLL)

    lax.fori_loop(0, cap, drain_cur, 0, unroll=DMA_LOOP_UNROLL)
    h = _rmsnorm_rows(xg[...], nw_ref[...]).astype(BF16)

    @pl.when(step + 1 < n_e * n_b)
    def _():
        _, b_next = _near_step(e, b, n_e, n_b, 1)
        issue_next, _ = gather(idx_next_ref, b_next * seq)
        lax.fori_loop(0, cap, issue_next, 0, unroll=DMA_LOOP_UNROLL)

    g = jnp.dot(h, wg_ref[...], preferred_element_type=F32)
    u = jnp.dot(h, wu_ref[...], preferred_element_type=F32)
    act_ref[...] = ((g * _sigmoid(g)) * u).astype(act_ref.dtype)


def _ffn1(idx, x2, nw, wg, wu, seq):
    bn, ne, cap = idx.shape
    d = x2.shape[1]
    ff = wg.shape[-1]
    idx4 = idx.reshape(bn, ne, 1, cap)
    return pl.pallas_call(
        functools.partial(_ffn1_kernel, seq=seq),
        grid=(ne, bn),
        in_specs=[
            _idx_spec(ne, bn, cap, 0), _idx_spec(ne, bn, cap, 1),
            pl.BlockSpec(memory_space=pl.ANY),
            pl.BlockSpec((1, d), lambda e, b: (0, 0)),
            pl.BlockSpec((None, d, ff), lambda e, b: (e, 0, 0)),
            pl.BlockSpec((None, d, ff), lambda e, b: (e, 0, 0)),
        ],
        out_specs=pl.BlockSpec((None, None, cap, ff), lambda e, b: (e, b, 0, 0)),
        out_shape=jax.ShapeDtypeStruct((ne, bn, cap, ff), BF16),
        scratch_shapes=[pltpu.VMEM((cap, d), F32), pltpu.SemaphoreType.DMA(())],
        compiler_params=_cparams("arbitrary", "arbitrary"),
        name="moe_ffn1",
    )(idx4, idx4, x2, nw.reshape(1, d), wg, wu)


def _ffn2_kernel(idx_ref, idx_next_ref, idx_prev_ref, gate_ref, act_ref, wd_ref, x_in, x_out,
                 xo, sem_in, sem_out, *, seq):
    del x_in
    cap = xo.shape[1]
    e, b = pl.program_id(0), pl.program_id(1)
    n_e, n_b = pl.num_programs(0), pl.num_programs(1)
    step = e * n_b + b
    slot = step % 2

    def loop(fn):
        lax.fori_loop(0, cap, fn, 0, unroll=DMA_LOOP_UNROLL)

    def gather(ir, base, sl):
        def copy(c):
            return _row_copy(x_out, xo.at[sl], base + ir[0, c], c, sem_in.at[sl])

        def issue(c, carry):
            copy(c).start()
            return carry

        def drain(c, carry):
            copy(c).wait()
            return carry

        return issue, drain

    def scatter(ir, base, sl):
        def copy(c):
            return _row_copy(xo.at[sl], x_out, c, base + ir[0, c], sem_out.at[sl])

        def issue(c, carry):
            copy(c).start()
            return carry

        def drain(c, carry):
            copy(c).wait()
            return carry

        return issue, drain

    issue_cur, drain_cur = gather(idx_ref, b * seq, slot)

    @pl.when(step == 0)
    def _():
        loop(issue_cur)

    ye = jnp.dot(act_ref[...], wd_ref[...], preferred_element_type=F32) * gate_ref[...]

    @pl.when(step > 0)
    def _():
        _, b_prev = _near_step(e, b, n_e, n_b, -1)
        loop(scatter(idx_prev_ref, b_prev * seq, 1 - slot)[1])

    @pl.when(step + 1 < n_e * n_b)
    def _():
        _, b_next = _near_step(e, b, n_e, n_b, 1)
        loop(gather(idx_next_ref, b_next * seq, 1 - slot)[0])

    loop(drain_cur)
    xo[slot] = xo[slot] + ye
    issue_put, drain_put = scatter(idx_ref, b * seq, slot)
    loop(issue_put)

    @pl.when(step + 1 == n_e * n_b)
    def _():
        loop(drain_put)


def _ffn2(idx, gate, act, wd, x2, seq):
    bn, ne, cap = idx.shape
    assert bn >= 2
    t, d = x2.shape
    ff = wd.shape[1]
    idx4 = idx.reshape(bn, ne, 1, cap)
    return pl.pallas_call(
        functools.partial(_ffn2_kernel, seq=seq),
        grid=(ne, bn),
        in_specs=[
            _idx_spec(ne, bn, cap, 0), _idx_spec(ne, bn, cap, 1), _idx_spec(ne, bn, cap, -1),
            pl.BlockSpec((None, None, cap, 1), lambda e, b: (b, e, 0, 0)),
            pl.BlockSpec((None, None, cap, ff), lambda e, b: (e, b, 0, 0)),
            pl.BlockSpec((None, ff, d), lambda e, b: (e, 0, 0)),
            pl.BlockSpec(memory_space=pl.ANY),
        ],
        out_specs=pl.BlockSpec(memory_space=pl.ANY),
        out_shape=jax.ShapeDtypeStruct((t, d), F32),
        input_output_aliases={6: 0},
        scratch_shapes=[pltpu.VMEM((2, cap, d), F32), pltpu.SemaphoreType.DMA((2,)),
                        pltpu.SemaphoreType.DMA((2,))],
        compiler_params=_cparams("arbitrary", "arbitrary"),
        name="moe_ffn2",
    )(idx4, idx4, idx4, gate, act, wd, x2)


def kernel(x, mem, positions, w_in, w_out, norm_mix_w, norm_cross_w, norm_mem_w, norm_ffn_w, final_norm_w, ret_decay, ret_gn_w, s5_lam_re, s5_lam_im, s5_log_dt, s5_b_re, s5_b_im, s5_c_re, s5_c_im, s5_d, s5_glu_w, cross_wq, cross_wkv, cross_wo, router_w, expert_w_gate, expert_w_up, expert_w_down):
    bn, s, d = x.shape
    depth = w_in.shape[0]
    n_mem = mem.shape[1]
    width_a = 3 * d // 8
    width_b = 3 * d // 8
    width_c = d - width_a - width_b
    n_heads_a = width_a // HEAD_DIM_A
    q_off_b = 3 * width_a
    u_off = q_off_b + 2 * N_HEADS_B * LANES + 2 * width_b
    capacity = EC_CAPACITY_FACTOR * s // N_EXPERTS
    t = bn * s

    ca, sa, cbt, sbt = _rope_tables(positions)
    x2 = x.reshape(t, d)
    mem2 = mem.reshape(bn * n_mem, d)
    for l in range(depth):
        h = _rmsnorm(x2, norm_mix_w[l], BF16)
        proj = _matmul(h, w_in[l].astype(BF16), F32)
        proj3 = proj.reshape(bn, s, -1)
        a_out = _attn_a(proj3, ca, sa, n_heads_a)
        b_out = _retention(proj3, cbt, sbt, ret_decay[l], ret_gn_w[l], q_off_b, N_HEADS_B)
        s5_tabs = _s5_tables(s5_lam_re[l], s5_lam_im[l], s5_log_dt[l], s5_b_re[l], s5_b_im[l],
                             s5_c_re[l], s5_c_im[l])
        ys = _s5(proj3[:, :, u_off:], s5_tabs)
        c_out = _s5_glu(ys, proj, u_off, s5_d[l], s5_glu_w[l].astype(BF16))
        x2 = _out_proj(a_out.reshape(t, width_a), b_out.reshape(t, width_b), c_out, w_out[l].astype(BF16), x2)

        mn = _rmsnorm(mem2, norm_mem_w[l], BF16)
        kv = _matmul(mn, cross_wkv[l].astype(BF16), BF16)
        a_fold, vw_fold = _cross_fold(kv.reshape(bn, n_mem, 2 * d), cross_wq[l].astype(BF16),
                                      cross_wo[l].astype(BF16), N_HEADS_X)
        x3 = x2.reshape(bn, s, d)
        probs = _cross_probs(x3, norm_cross_w[l], a_fold, N_HEADS_X)
        x2 = _bmm_res(probs, vw_fold, x3).reshape(t, d)

        logits = _router_logits(x2, norm_ffn_w[l], router_w[l].T.astype(BF16))
        idx, gate = _topk(logits, bn, capacity)
        act = _ffn1(idx, x2, norm_ffn_w[l], expert_w_gate[l].astype(BF16), expert_w_up[l].astype(BF16), s)
        x2 = _ffn2(idx, gate, act, expert_w_down[l].astype(BF16), x2, s)
    return _rmsnorm(x2, final_norm_w, x.dtype).reshape(bn, s, d)
```

```python
import functools
import math

import numpy as np
import jax
import jax.numpy as jnp
from jax import lax
from jax.experimental import pallas as pl
from jax.experimental.pallas import tpu as pltpu

F32 = jnp.float32
BF16 = jnp.bfloat16

EPS = 1e-6
NEG_INF = -1e30
VMEM_LIMIT_BYTES = 56 * 1024 * 1024
LANES = 128

HEAD_DIM_A = 128
ROT_DIM_A = HEAD_DIM_A // 4
ROPE_THETA = 500000.0
DILATION_PAIRS = ((128, 1), (512, 4), (2048, 16))
ATT_BLK = 128
N_HEADS_B = 6
RET_CHUNK = 128
RET_THETA = 10000.0
S5_GROUP = 16
S5_STATE = 64
S5_CHUNK = 32
N_HEADS_X = 4
N_EXPERTS = 16
EC_CAPACITY_FACTOR = 2


def _cparams(*sem):
    return pltpu.CompilerParams(dimension_semantics=sem, vmem_limit_bytes=VMEM_LIMIT_BYTES)


def _sigmoid(x):
    return 1.0 / (1.0 + jnp.exp(-x))


def _rmsnorm_rows(x, w):
    ms = jnp.mean(x * x, axis=-1, keepdims=True)
    return x * lax.rsqrt(ms + EPS) * w


def _rmsnorm_kernel(x_ref, w_ref, o_ref):
    o_ref[...] = _rmsnorm_rows(x_ref[...], w_ref[...]).astype(o_ref.dtype)


def _rmsnorm(x2, w, out_dtype, tm=512):
    t, d = x2.shape
    tm = min(tm, t)
    return pl.pallas_call(
        _rmsnorm_kernel,
        grid=(t // tm,),
        in_specs=[pl.BlockSpec((tm, d), lambda i: (i, 0)), pl.BlockSpec((1, d), lambda i: (0, 0))],
        out_specs=pl.BlockSpec((tm, d), lambda i: (i, 0)),
        out_shape=jax.ShapeDtypeStruct((t, d), out_dtype),
        compiler_params=_cparams("parallel"),
        name="rmsnorm",
    )(x2, w.reshape(1, d))


def _router_kernel(x_ref, w_ref, rw_ref, lg_ref):
    h = _rmsnorm_rows(x_ref[...], w_ref[...]).astype(BF16)
    lg_ref[...] = lax.dot_general(rw_ref[...], h, (((1,), (1,)), ((), ())), preferred_element_type=F32)


def _router_logits(x2, w, rw_t, tm=512):
    t, d = x2.shape
    ne = rw_t.shape[0]
    return pl.pallas_call(
        _router_kernel,
        grid=(t // tm,),
        in_specs=[
            pl.BlockSpec((tm, d), lambda i: (i, 0)),
            pl.BlockSpec((1, d), lambda i: (0, 0)),
            pl.BlockSpec((ne, d), lambda i: (0, 0)),
        ],
        out_specs=pl.BlockSpec((ne, tm), lambda i: (0, i)),
        out_shape=jax.ShapeDtypeStruct((ne, t), F32),
        compiler_params=_cparams("parallel"),
        name="router_logits",
    )(x2, w.reshape(1, d), rw_t)


def _mm_ws_kernel(a_ref, w_ref, o_ref, wb):
    @pl.when(pl.program_id(1) == 0)
    def _():
        wb[...] = w_ref[...].astype(BF16)

    o_ref[...] = jnp.dot(a_ref[...], wb[...], preferred_element_type=F32).astype(o_ref.dtype)


def _mm3_ws_kernel(a1_ref, a2_ref, a3_ref, w_ref, r_ref, o_ref, wb):
    @pl.when(pl.program_id(1) == 0)
    def _():
        wb[...] = w_ref[...].astype(BF16)

    k1 = a1_ref.shape[1]
    k2 = a2_ref.shape[1]
    acc = jnp.dot(a1_ref[...], wb[0:k1, :], preferred_element_type=F32)
    acc += jnp.dot(a2_ref[...], wb[k1:k1 + k2, :], preferred_element_type=F32)
    acc += jnp.dot(a3_ref[...], wb[k1 + k2:, :], preferred_element_type=F32)
    o_ref[...] = r_ref[...] + acc


def _matmul(a, w3, layer, out_dtype, tm=1024, tn=512):
    m, k = a.shape
    n = w3.shape[2]
    tm = min(tm, m)
    tn = min(tn, n)
    return pl.pallas_call(
        _mm_ws_kernel,
        grid=(n // tn, m // tm),
        in_specs=[pl.BlockSpec((tm, k), lambda j, i: (i, 0)),
                  pl.BlockSpec((None, k, tn), lambda j, i: (layer, 0, j))],
        out_specs=pl.BlockSpec((tm, tn), lambda j, i: (i, j)),
        out_shape=jax.ShapeDtypeStruct((m, n), out_dtype),
        scratch_shapes=[pltpu.VMEM((k, tn), BF16)],
        compiler_params=_cparams("parallel", "arbitrary"),
        name="matmul",
    )(a, w3)


def _out_proj(a1, a2, a3, w3, layer, res, tm=1024, tn=512):
    m = a1.shape[0]
    k1, k2, k3 = a1.shape[1], a2.shape[1], a3.shape[1]
    k, n = w3.shape[1], w3.shape[2]
    assert k1 + k2 + k3 == k
    return pl.pallas_call(
        _mm3_ws_kernel,
        grid=(n // tn, m // tm),
        in_specs=[
            pl.BlockSpec((tm, k1), lambda j, i: (i, 0)),
            pl.BlockSpec((tm, k2), lambda j, i: (i, 0)),
            pl.BlockSpec((tm, k3), lambda j, i: (i, 0)),
            pl.BlockSpec((None, k, tn), lambda j, i: (layer, 0, j)),
            pl.BlockSpec((tm, tn), lambda j, i: (i, j)),
        ],
        out_specs=pl.BlockSpec((tm, tn), lambda j, i: (i, j)),
        out_shape=jax.ShapeDtypeStruct((m, n), F32),
        scratch_shapes=[pltpu.VMEM((k, tn), BF16)],
        compiler_params=_cparams("parallel", "arbitrary"),
        name="out_proj",
    )(a1, a2, a3, w3, res)


def _rope_kernel(pos_ref, fa_ref, fb_ref, ca_ref, sa_ref, cb_ref, sb_ref):
    pos = pos_ref[...].astype(F32)
    ang_a = pos * fa_ref[...]
    ca_ref[...] = jnp.cos(ang_a)
    sa_ref[...] = jnp.sin(ang_a)
    ang_b = pos * fb_ref[...]
    lane = lax.broadcasted_iota(jnp.int32, ang_b.shape, 1)
    sin_b = jnp.sin(ang_b)
    cb_ref[...] = jnp.cos(ang_b)
    sb_ref[...] = jnp.where(lane < LANES // 2, -sin_b, sin_b)


def _rope_tables(positions):
    b, s = positions.shape
    half_a = ROT_DIM_A // 2
    fa = ROPE_THETA ** (-(jnp.arange(half_a, dtype=F32) * 2.0 / ROT_DIM_A))
    fa_row = jnp.concatenate([fa, fa, jnp.zeros((LANES - 2 * half_a,), F32)]).reshape(1, LANES)
    fb = RET_THETA ** (-jnp.linspace(0.0, 1.0, LANES // 2, dtype=F32))
    fb_row = jnp.concatenate([fb, fb]).reshape(1, LANES)
    tab = jax.ShapeDtypeStruct((b, s, LANES), F32)
    row = pl.BlockSpec((1, LANES), lambda i: (0, 0))
    out = pl.BlockSpec((None, s, LANES), lambda i: (i, 0, 0))
    return pl.pallas_call(
        _rope_kernel,
        grid=(b,),
        in_specs=[pl.BlockSpec((None, s, 1), lambda i: (i, 0, 0)), row, row],
        out_specs=[out, out, out, out],
        out_shape=[tab, tab, tab, tab],
        compiler_params=_cparams("parallel"),
        name="rope_tables",
    )(positions.reshape(b, s, 1), fa_row, fb_row)


def _attn_a_kernel(q_ref, k_ref, v_ref, c_ref, s_ref, o_ref, qn, kn, accn, mn, ln, *, seq, radius):
    rc = 128
    scale = HEAD_DIM_A ** -0.5
    half = ROT_DIM_A // 2

    def rot_body(i, carry):
        rows = pl.ds(pl.multiple_of(i * rc, rc), rc)
        lane = lax.broadcasted_iota(jnp.int32, (rc, LANES), 1)
        c = c_ref[rows, :]
        s = s_ref[rows, :]
        s_lo = jnp.where(lane < half, -s, 0.0)
        s_hi = jnp.where(lane >= half, s, 0.0)
        q = q_ref[rows, :]
        k = k_ref[rows, :]
        qn[rows, :] = (q * c + pltpu.roll(q, LANES - half, 1) * s_lo + pltpu.roll(q, half, 1) * s_hi) * scale
        kn[rows, :] = k * c + pltpu.roll(k, LANES - half, 1) * s_lo + pltpu.roll(k, half, 1) * s_hi
        return carry

    lax.fori_loop(0, seq // rc, rot_body, 0, unroll=2)

    for branch, (_, dil) in enumerate(DILATION_PAIRS):
        sub = seq // dil
        blk = sub if sub <= 2 * ATT_BLK else ATT_BLK
        win = min(sub, blk + 2 * radius)
        nbr = sub // blk
        dmat = (lax.broadcasted_iota(jnp.int32, (blk, win), 1)
                - lax.broadcasted_iota(jnp.int32, (blk, win), 0))

        def rows_of(start, size, dil=dil):
            return pl.ds(start, size) if dil == 1 else pl.ds(start, size, stride=dil)

        def blk_body(n, carry, branch=branch, dil=dil, sub=sub, nbr=nbr, rows_of=rows_of,
                     blk=blk, win=win, dmat=dmat):
            r = n // nbr
            j = n - r * nbr
            wrel = jnp.clip(j * blk - radius, 0, sub - win)
            q_rows = rows_of(r + dil * (j * blk), blk)
            k_rows = rows_of(r + dil * wrel, win)
            qb = qn[q_rows, :].astype(BF16)
            kw = kn[k_rows, :].astype(BF16)
            vw = v_ref[k_rows, :].astype(BF16)
            sc = lax.dot_general(qb, kw, (((1,), (1,)), ((), ())), preferred_element_type=F32)
            sc = jnp.where(jnp.abs(dmat + (wrel - j * blk)) <= radius, sc, NEG_INF)
            m = jnp.max(sc, axis=1, keepdims=True)
            p = jnp.exp(sc - m)
            l = jnp.broadcast_to(jnp.sum(p, axis=1, keepdims=True), (blk, LANES))
            acc = jnp.dot(p.astype(BF16), vw, preferred_element_type=F32)
            m = jnp.broadcast_to(m, (blk, LANES))
            if branch > 0:
                m_old = mn[q_rows, :]
                m_max = jnp.maximum(m_old, m)
                e_old = jnp.exp(m_old - m_max)
                e_new = jnp.exp(m - m_max)
                acc = accn[q_rows, :] * e_old + acc * e_new
                l = ln[q_rows, :] * e_old + l * e_new
                m = m_max
            accn[q_rows, :] = acc
            mn[q_rows, :] = m
            ln[q_rows, :] = l
            return carry

        lax.fori_loop(0, seq // blk, blk_body, 0, unroll=8 if blk == ATT_BLK else 4)

    def out_body(i, carry):
        rows = pl.ds(pl.multiple_of(i * rc, rc), rc)
        o_ref[rows, :] = (accn[rows, :] / ln[rows, :]).astype(o_ref.dtype)
        return carry

    lax.fori_loop(0, seq // rc, out_body, 0, unroll=4)


def _attn_a(proj3, ca, sa, n_heads):
    b, s, _ = proj3.shape
    radii = {w // (2 * d) for w, d in DILATION_PAIRS}
    assert len(radii) == 1
    radius = radii.pop()
    for _, d in DILATION_PAIRS:
        assert s % (d * ATT_BLK) == 0 and s // d >= 2 * ATT_BLK

    def col(off):
        return pl.BlockSpec((None, s, HEAD_DIM_A), lambda i, h: (i, 0, off + h))

    tab = pl.BlockSpec((None, s, LANES), lambda i, h: (i, 0, 0))
    f32s = lambda: pltpu.VMEM((s, LANES), F32)
    return pl.pallas_call(
        functools.partial(_attn_a_kernel, seq=s, radius=radius),
        grid=(b, n_heads),
        in_specs=[col(0), col(n_heads), col(2 * n_heads), tab, tab],
        out_specs=pl.BlockSpec((None, s, HEAD_DIM_A), lambda i, h: (i, 0, h)),
        out_shape=jax.ShapeDtypeStruct((b, s, n_heads * HEAD_DIM_A), BF16),
        scratch_shapes=[f32s(), f32s(), f32s(), f32s(), f32s()],
        compiler_params=_cparams("parallel", "parallel"),
        name="attn_a",
    )(proj3, proj3, proj3, ca, sa)


def _ret_kernel(q_ref, k_ref, v_ref, g_ref, c_ref, s_ref, dm_ref, dec_ref, cdec_ref, gn_ref, o_ref,
                qn, kn, racc, sf, sb, *, seq):
    cb = RET_CHUNK
    nc = seq // cb
    dk = q_ref.shape[-1]
    dv = v_ref.shape[-1]
    h = pl.program_id(1)
    kscale = dk ** -0.5

    def rot_body(i, carry):
        rows = pl.ds(pl.multiple_of(i * cb, cb), cb)
        c = c_ref[rows, :]
        s = s_ref[rows, :]
        q = q_ref[rows, :]
        k = k_ref[rows, :]
        qn[rows, :] = q * c + pltpu.roll(q, dk // 2, 1) * s
        kn[rows, :] = (k * c + pltpu.roll(k, dk // 2, 1) * s) * kscale
        racc[rows, :] = jnp.zeros((cb, dv), F32)
        return carry

    lax.fori_loop(0, nc, rot_body, 0, unroll=2)
    sf[...] = jnp.zeros(sf.shape, F32)
    sb[...] = jnp.zeros(sb.shape, F32)
    c_f = cdec_ref[h, 0]
    c_b = cdec_ref[h, 1]

    def body(n, carry):
        rows_f = pl.ds(pl.multiple_of(n * cb, cb), cb)
        rows_b = pl.ds(pl.multiple_of((nc - 1 - n) * cb, cb), cb)
        q = qn[rows_f, :]
        k = kn[rows_f, :]
        v = v_ref[rows_f, :].astype(BF16)
        sc = lax.dot_general(q.astype(BF16), k.astype(BF16), (((1,), (1,)), ((), ())),
                             preferred_element_type=F32) * dm_ref[...]
        intra = jnp.dot(sc.astype(BF16), v, preferred_element_type=F32)
        cross = jnp.dot((q * dec_ref[0]).astype(BF16), sf[...].astype(BF16), preferred_element_type=F32)
        racc[rows_f, :] += intra + cross
        kd = (k * dec_ref[1]).T.astype(BF16)
        sf[...] = c_f * sf[...] + jnp.dot(kd, v, preferred_element_type=F32)
        q2 = qn[rows_b, :]
        k2 = kn[rows_b, :]
        v2 = v_ref[rows_b, :].astype(BF16)
        cross_b = jnp.dot((q2 * dec_ref[2]).astype(BF16), sb[...].astype(BF16), preferred_element_type=F32)
        racc[rows_b, :] += cross_b
        kd2 = (k2 * dec_ref[3]).T.astype(BF16)
        sb[...] = c_b * sb[...] + jnp.dot(kd2, v2, preferred_element_type=F32)
        return carry

    lax.fori_loop(0, nc, body, 0, unroll=2)

    def out_body(i, carry):
        rows = pl.ds(pl.multiple_of(i * cb, cb), cb)
        r = racc[rows, :]
        mu = jnp.mean(r, axis=-1, keepdims=True)
        xc = r - mu
        var = jnp.mean(xc * xc, axis=-1, keepdims=True)
        y = xc * lax.rsqrt(var + EPS) * gn_ref[...]
        g = g_ref[rows, :]
        o_ref[rows, :] = (y * (g * _sigmoid(g))).astype(o_ref.dtype)
        return carry

    lax.fori_loop(0, nc, out_body, 0, unroll=4)


def _ret_tables(ret_decay_l):
    cb = RET_CHUNK
    log_g = jax.nn.log_sigmoid(ret_decay_l.astype(F32))
    lf = log_g[0][:, None, None]
    lb = log_g[1][:, None, None]
    i = jnp.arange(cb)[:, None]
    j = jnp.arange(cb)[None, :]
    dfwd = jnp.where(i >= j, i - j, 0).astype(F32)[None]
    dbwd = jnp.where(j > i, j - i, 0).astype(F32)[None]
    dmask = jnp.where((i >= j)[None], jnp.exp(dfwd * lf), jnp.exp(dbwd * lb))
    pos = jnp.arange(cb, dtype=F32)[None, :]
    lf2, lb2 = log_g[0][:, None], log_g[1][:, None]
    dec = jnp.stack([jnp.exp((pos + 1.0) * lf2), jnp.exp((cb - 1.0 - pos) * lf2),
                     jnp.exp((cb - pos) * lb2), jnp.exp(pos * lb2)], axis=1)
    dec = jnp.broadcast_to(dec[..., None], dec.shape + (LANES,))
    cdec = jnp.exp(cb * log_g).T
    return dmask, dec, cdec


def _retention(proj3, cbt, sbt, ret_decay_l, gn_w, q_off, n_heads):
    b, s, _ = proj3.shape
    dk = LANES
    dv = 2 * dk
    dmask, dec, cdec = _ret_tables(ret_decay_l)
    qb0 = q_off // dk
    kb0 = qb0 + n_heads
    vb0 = (q_off + 2 * n_heads * dk) // dv
    gb0 = vb0 + n_heads
    tab = pl.BlockSpec((None, s, LANES), lambda i, h: (i, 0, 0))
    return pl.pallas_call(
        functools.partial(_ret_kernel, seq=s),
        grid=(b, n_heads),
        in_specs=[
            pl.BlockSpec((None, s, dk), lambda i, h: (i, 0, qb0 + h)),
            pl.BlockSpec((None, s, dk), lambda i, h: (i, 0, kb0 + h)),
            pl.BlockSpec((None, s, dv), lambda i, h: (i, 0, vb0 + h)),
            pl.BlockSpec((None, s, dv), lambda i, h: (i, 0, gb0 + h)),
            tab, tab,
            pl.BlockSpec((None, RET_CHUNK, RET_CHUNK), lambda i, h: (h, 0, 0)),
            pl.BlockSpec((None, 4, RET_CHUNK, LANES), lambda i, h: (h, 0, 0, 0)),
            pl.BlockSpec(memory_space=pltpu.SMEM),
            pl.BlockSpec((None, 1, dv), lambda i, h: (h, 0, 0)),
        ],
        out_specs=pl.BlockSpec((None, s, dv), lambda i, h: (i, 0, h)),
        out_shape=jax.ShapeDtypeStruct((b, s, n_heads * dv), BF16),
        scratch_shapes=[pltpu.VMEM((s, dk), F32), pltpu.VMEM((s, dk), F32), pltpu.VMEM((s, dv), F32),
                        pltpu.VMEM((dk, dv), F32), pltpu.VMEM((dk, dv), F32)],
        compiler_params=_cparams("parallel", "parallel"),
        name="retention",
    )(proj3, proj3, proj3, proj3, cbt, sbt, dmask, dec, cdec, gn_w.reshape(n_heads, 1, dv))


def _s5_kernel(u_ref, strip_ref, e_ref, f_ref, a_ref, y_ref, kv, hp, m_sc, *, n_chunks, bn):
    p = S5_STATE
    c = S5_GROUP
    L = S5_CHUNK
    strip = strip_ref[...]
    for j in range(L):
        m_sc[j * c:(j + 1) * c, :] = strip[:, (L - 1 - j) * c:(2 * L - 1 - j) * c].astype(BF16)
    u = u_ref[...]
    kv[...] = jnp.dot(u, e_ref[...], preferred_element_type=F32)
    a_re = a_ref[0:1, :]
    a_im = a_ref[1:2, :]
    is_fwd = lax.broadcasted_iota(jnp.int32, (bn, 2 * p), 1) < p
    h_re = jnp.zeros((bn, 2 * p), F32)
    h_im = jnp.zeros((bn, 2 * p), F32)
    for n in range(n_chunks):
        rf = pl.ds(n * bn, bn)
        rb = pl.ds((n_chunks - 1 - n) * bn, bn)
        hp[rf, 0:p] = h_re[:, 0:p]
        hp[rb, p:2 * p] = h_re[:, p:2 * p]
        hp[rf, 2 * p:3 * p] = h_im[:, 0:p]
        hp[rb, 3 * p:4 * p] = h_im[:, p:2 * p]
        kv_re = jnp.where(is_fwd, kv[rf, 0:2 * p], kv[rb, 0:2 * p])
        kv_im = jnp.where(is_fwd, kv[rf, 2 * p:4 * p], kv[rb, 2 * p:4 * p])
        h_re, h_im = (a_re * h_re - a_im * h_im + kv_re, a_re * h_im + a_im * h_re + kv_im)
    y = jnp.dot(u, m_sc[...], preferred_element_type=F32)
    y_ref[...] = y + jnp.dot(hp[...].astype(BF16), f_ref[...], preferred_element_type=F32)


def _s5_tables(lam_re, lam_im, log_dt, b_re, b_im, c_re, c_im):
    hp = lax.Precision.HIGHEST
    L = S5_CHUNK
    lam_re = lam_re.astype(F32)
    lam_im = lam_im.astype(F32)
    dt = jnp.exp(log_dt.astype(F32))[..., None]
    mag = jnp.exp(lam_re * dt)
    ar = mag * jnp.cos(lam_im * dt)
    ai = mag * jnp.sin(lam_im * dt)
    den = lam_re * lam_re + lam_im * lam_im
    nr = ar - 1.0
    cr = (nr * lam_re + ai * lam_im) / den
    ci = (ai * lam_re - nr * lam_im) / den
    b_re = b_re.astype(F32)
    b_im = b_im.astype(F32)
    bbr = cr[..., None] * b_re - ci[..., None] * b_im
    bbi = cr[..., None] * b_im + ci[..., None] * b_re
    c_re = c_re.astype(F32)
    c_im = c_im.astype(F32)
    tau = jnp.arange(L + 1, dtype=F32)[:, None, None, None]
    pmag = jnp.exp(tau * (lam_re * dt)[None])
    pang = tau * (lam_im * dt)[None]
    pr = pmag * jnp.cos(pang)
    pi = pmag * jnp.sin(pang)
    car = c_re[None] * pr[:L, :, :, None, :] - c_im[None] * pi[:L, :, :, None, :]
    cai = c_re[None] * pi[:L, :, :, None, :] + c_im[None] * pr[:L, :, :, None, :]
    kk = (jnp.einsum('tdgop,dgpc->tdgoc', car, bbr, precision=hp)
          - jnp.einsum('tdgop,dgpc->tdgoc', cai, bbi, precision=hp))
    g = lam_re.shape[1]
    c = b_re.shape[-1]
    lags = jnp.concatenate([kk[1:, 1][::-1], (kk[0, 0] + kk[0, 1])[None], kk[1:, 0]], axis=0)
    strip = lags.transpose(1, 3, 0, 2).reshape(g, c, (2 * L - 1) * c)
    strip = jnp.pad(strip, ((0, 0), (0, 0), (0, c)))
    pf_r, pf_i = pr[:L, 0][::-1], pi[:L, 0][::-1]
    pb_r, pb_i = pr[:L, 1], pi[:L, 1]
    def bmul(xr, xi, d):
        re = xr[..., None] * bbr[d][None] - xi[..., None] * bbi[d][None]
        im = xr[..., None] * bbi[d][None] + xi[..., None] * bbr[d][None]
        return re.transpose(1, 0, 3, 2), im.transpose(1, 0, 3, 2)
    ef_r, ef_i = bmul(pf_r, pf_i, 0)
    eb_r, eb_i = bmul(pb_r, pb_i, 1)
    e_m = jnp.concatenate([ef_r, eb_r, ef_i, eb_i], axis=-1).reshape(g, L * c, 4 * S5_STATE)
    def cmul(xr, xi, d):
        re = c_re[d][None] * xr[:, :, None, :] - c_im[d][None] * xi[:, :, None, :]
        im = c_re[d][None] * xi[:, :, None, :] + c_im[d][None] * xr[:, :, None, :]
        return re.transpose(1, 3, 0, 2), -im.transpose(1, 3, 0, 2)
    ff_r, ff_i = cmul(pr[1:L + 1, 0], pi[1:L + 1, 0], 0)
    fb_r, fb_i = cmul(pr[1:L + 1, 1][::-1], pi[1:L + 1, 1][::-1], 1)
    f_m = jnp.concatenate([ff_r, fb_r, ff_i, fb_i], axis=1).reshape(g, 4 * S5_STATE, L * c)
    a_l = jnp.stack([jnp.concatenate([pr[L, 0], pr[L, 1]], axis=-1),
                     jnp.concatenate([pi[L, 0], pi[L, 1]], axis=-1)], axis=1)
    return strip, e_m.astype(BF16), f_m.astype(BF16), a_l


def _s5(uc3, tables):
    strip, e_m, f_m, a_l = tables
    b, s, w = uc3.shape
    c = S5_GROUP
    g = w // c
    L = S5_CHUNK
    nc = s // L
    rows = nc * b
    ug = uc3.reshape(b, nc, L, g, c).transpose(3, 1, 0, 2, 4).reshape(g, rows, L * c).astype(BF16)
    ys = pl.pallas_call(
        functools.partial(_s5_kernel, n_chunks=nc, bn=b),
        grid=(g,),
        in_specs=[
            pl.BlockSpec((None, rows, L * c), lambda i: (i, 0, 0)),
            pl.BlockSpec((None, c, 2 * L * c), lambda i: (i, 0, 0)),
            pl.BlockSpec((None, L * c, 4 * S5_STATE), lambda i: (i, 0, 0)),
            pl.BlockSpec((None, 4 * S5_STATE, L * c), lambda i: (i, 0, 0)),
            pl.BlockSpec((None, 2, 2 * S5_STATE), lambda i: (i, 0, 0)),
        ],
        out_specs=pl.BlockSpec((None, rows, L * c), lambda i: (i, 0, 0)),
        out_shape=jax.ShapeDtypeStruct((g, rows, L * c), F32),
        scratch_shapes=[pltpu.VMEM((rows, 4 * S5_STATE), F32), pltpu.VMEM((rows, 4 * S5_STATE), F32),
                        pltpu.VMEM((L * c, L * c), BF16)],
        compiler_params=_cparams("parallel"),
        name="s5",
    )(ug, strip, e_m, f_m, a_l)
    return ys.reshape(g, nc, b, L, c).transpose(2, 1, 3, 0, 4).reshape(b * s, w)


def _glu_kernel(ys_ref, u_ref, d_ref, w_ref, o_ref):
    y = ys_ref[...] + u_ref[...] * d_ref[...]
    y = 0.5 * y * (1.0 + jnp.tanh(math.sqrt(2.0 / math.pi) * (y + 0.044715 * (y * y * y))))
    z = jnp.dot(y.astype(BF16), w_ref[...], preferred_element_type=F32)
    o_ref[...] = (y * _sigmoid(z)).astype(o_ref.dtype)


def _s5_glu(ys, proj2, u_off, d_w, glu_w, tm=512):
    t, w = ys.shape
    ub = u_off // w
    return pl.pallas_call(
        _glu_kernel,
        grid=(t // tm,),
        in_specs=[
            pl.BlockSpec((tm, w), lambda i: (i, 0)),
            pl.BlockSpec((tm, w), lambda i: (i, ub)),
            pl.BlockSpec((1, w), lambda i: (0, 0)),
            pl.BlockSpec((w, w), lambda i: (0, 0)),
        ],
        out_specs=pl.BlockSpec((tm, w), lambda i: (i, 0)),
        out_shape=jax.ShapeDtypeStruct((t, w), BF16),
        compiler_params=_cparams("parallel"),
        name="s5_glu",
    )(ys, proj2, d_w.reshape(1, w), glu_w)


def _cross_qk_kernel(wq_ref, km_ref, a_ref, wb):
    @pl.when(pl.program_id(1) == 0)
    def _():
        wb[...] = wq_ref[...].astype(BF16)

    a_ref[...] = lax.dot_general(wb[...], km_ref[...], (((1,), (1,)), ((), ())),
                                 preferred_element_type=F32).astype(a_ref.dtype)


def _cross_vo_kernel(vm_ref, wo_ref, vw_ref, wb):
    @pl.when(pl.program_id(1) == 0)
    def _():
        wb[...] = wo_ref[...].astype(BF16)

    vw_ref[...] = jnp.dot(vm_ref[...], wb[...], preferred_element_type=F32).astype(vw_ref.dtype)


def _cross_fold(kv3, wq3, wo3, layer, n_heads):
    b, nm, d2 = kv3.shape
    d = d2 // 2
    hd = d // n_heads
    a = pl.pallas_call(
        _cross_qk_kernel,
        grid=(n_heads, b),
        in_specs=[pl.BlockSpec((None, d, hd), lambda h, i: (layer, 0, h)),
                  pl.BlockSpec((None, nm, hd), lambda h, i: (i, 0, h))],
        out_specs=pl.BlockSpec((None, d, nm), lambda h, i: (i, 0, h)),
        out_shape=jax.ShapeDtypeStruct((b, d, n_heads * nm), BF16),
        scratch_shapes=[pltpu.VMEM((d, hd), BF16)],
        compiler_params=_cparams("parallel", "arbitrary"),
        name="cross_fold_qk",
    )(wq3, kv3)
    vw = pl.pallas_call(
        _cross_vo_kernel,
        grid=(n_heads, b),
        in_specs=[pl.BlockSpec((None, nm, hd), lambda h, i: (i, 0, n_heads + h)),
                  pl.BlockSpec((None, hd, d), lambda h, i: (layer, h, 0))],
        out_specs=pl.BlockSpec((None, nm, d), lambda h, i: (i, h, 0)),
        out_shape=jax.ShapeDtypeStruct((b, n_heads * nm, d), BF16),
        scratch_shapes=[pltpu.VMEM((hd, d), BF16)],
        compiler_params=_cparams("parallel", "arbitrary"),
        name="cross_fold_vo",
    )(kv3, wo3)
    return a, vw


def _cross_probs_kernel(x_ref, nw_ref, a_ref, p_ref, *, n_heads, scale):
    h = _rmsnorm_rows(x_ref[...], nw_ref[...]).astype(BF16)
    sc = jnp.dot(h, a_ref[...], preferred_element_type=F32) * scale
    nm = sc.shape[1] // n_heads
    for hh in range(n_heads):
        s_h = sc[:, hh * nm:(hh + 1) * nm]
        m = jnp.max(s_h, axis=-1, keepdims=True)
        e = jnp.exp(s_h - m)
        p_ref[:, hh * nm:(hh + 1) * nm] = (e / jnp.sum(e, axis=-1, keepdims=True)).astype(p_ref.dtype)


def _cross_probs(x3, nw, a, n_heads, tq=512):
    b, s, d = x3.shape
    hm = a.shape[2]
    scale = (d // n_heads) ** -0.5
    return pl.pallas_call(
        functools.partial(_cross_probs_kernel, n_heads=n_heads, scale=scale),
        grid=(b, s // tq),
        in_specs=[pl.BlockSpec((None, tq, d), lambda i, j: (i, j, 0)),
                  pl.BlockSpec((1, d), lambda i, j: (0, 0)),
                  pl.BlockSpec((None, d, hm), lambda i, j: (i, 0, 0))],
        out_specs=pl.BlockSpec((None, tq, hm), lambda i, j: (i, j, 0)),
        out_shape=jax.ShapeDtypeStruct((b, s, hm), BF16),
        compiler_params=_cparams("parallel", "parallel"),
        name="cross_probs",
    )(x3, nw.reshape(1, d), a)


def _bmm_res_kernel(a_ref, w_ref, r_ref, o_ref):
    o_ref[...] = r_ref[...] + jnp.dot(a_ref[...], w_ref[...], preferred_element_type=F32)


def _bmm_res(a3, w3, res3, tm=1024, tn=1024):
    b, s, k = a3.shape
    n = w3.shape[2]
    return pl.pallas_call(
        _bmm_res_kernel,
        grid=(b, s // tm, n // tn),
        in_specs=[pl.BlockSpec((None, tm, k), lambda i, j, c: (i, j, 0)),
                  pl.BlockSpec((None, k, tn), lambda i, j, c: (i, 0, c)),
                  pl.BlockSpec((None, tm, tn), lambda i, j, c: (i, j, c))],
        out_specs=pl.BlockSpec((None, tm, tn), lambda i, j, c: (i, j, c)),
        out_shape=jax.ShapeDtypeStruct((b, s, n), F32),
        compiler_params=_cparams("parallel", "parallel", "parallel"),
        name="cross_out",
    )(a3, w3, res3)


def _prefix_sum_lanes(x):
    n = x.shape[-1]
    lane = lax.broadcasted_iota(jnp.int32, x.shape, x.ndim - 1)
    k = 1
    while k < n:
        x = x + jnp.where(lane >= k, pltpu.roll(x, k, x.ndim - 1), 0.0)
        k *= 2
    return x


def _topk_kernel(lg_ref, idx_ref, gate_ref, *, capacity):
    lg = lg_ref[...]
    ne, s = lg.shape
    mx = jnp.max(lg, axis=0, keepdims=True)
    ex = jnp.exp(lg - mx)
    aff = ex / jnp.sum(ex, axis=0, keepdims=True)
    bits = pltpu.bitcast(aff, jnp.int32)

    def bit_body(i, thr):
        cand = thr | jnp.left_shift(jnp.int32(1), 30 - i)
        cnt = jnp.sum((bits >= cand).astype(F32), axis=1, keepdims=True)
        return jnp.where(cnt >= capacity, cand, thr)

    thr = lax.fori_loop(0, 31, bit_body, jnp.zeros((ne, 1), jnp.int32))
    gt = bits > thr
    eq = bits == thr
    n_gt = jnp.sum(gt.astype(F32), axis=1, keepdims=True)
    eq_rank = _prefix_sum_lanes(eq.astype(F32))
    sel = gt | (eq & (eq_rank <= capacity - n_gt))
    slot = _prefix_sum_lanes(sel.astype(F32)) - 1.0
    key = jnp.where(sel, slot, -1.0)
    tok = lax.broadcasted_iota(jnp.int32, (1, s), 1).astype(F32)
    lc = min(1024, s)
    for e in range(ne):
        idx_acc = jnp.zeros((capacity, 1), F32)
        gate_acc = jnp.zeros((capacity, 1), F32)
        for c0 in range(0, s, lc):
            slots = lax.broadcasted_iota(jnp.int32, (capacity, lc), 0).astype(F32)
            hit = key[e:e + 1, c0:c0 + lc] == slots
            idx_acc += jnp.sum(jnp.where(hit, tok[:, c0:c0 + lc], 0.0), axis=1, keepdims=True)
            gate_acc += jnp.sum(jnp.where(hit, aff[e:e + 1, c0:c0 + lc], 0.0), axis=1, keepdims=True)
        idx_ref[e] = idx_acc.astype(jnp.int32)
        gate_ref[e] = gate_acc


def _topk(logits, bn, capacity):
    ne, t = logits.shape
    s = t // bn
    idx, gate = pl.pallas_call(
        functools.partial(_topk_kernel, capacity=capacity),
        grid=(bn,),
        in_specs=[pl.BlockSpec((ne, s), lambda i: (0, i))],
        out_specs=[pl.BlockSpec((None, ne, capacity, 1), lambda i: (i, 0, 0, 0)),
                   pl.BlockSpec((None, ne, capacity, 1), lambda i: (i, 0, 0, 0))],
        out_shape=[jax.ShapeDtypeStruct((bn, ne, capacity, 1), jnp.int32),
                   jax.ShapeDtypeStruct((bn, ne, capacity, 1), F32)],
        compiler_params=_cparams("parallel"),
        name="topk",
    )(logits)
    return idx.reshape(bn, ne, capacity), gate


def _row_copy(src, dst, src_row, dst_row, sem):
    return pltpu.make_async_copy(src.at[pl.ds(src_row, 1), :], dst.at[pl.ds(dst_row, 1), :], sem)


DMA_LOOP_UNROLL = 8


def _near_step(e, b, n_e, n_b, delta):
    lin = (e * n_b + b + delta + n_e * n_b) % (n_e * n_b)
    return lin // n_b, lin % n_b


def _idx_spec(n_e, n_b, cap, delta):
    def index_map(e, b):
        e2, b2 = _near_step(e, b, n_e, n_b, delta)
        return (b2, e2, 0, 0)

    return pl.BlockSpec((None, None, 1, cap), index_map, memory_space=pltpu.SMEM)


def _ffn1_kernel(idx_ref, idx_next_ref, x_hbm, nw_ref, wg_ref, wu_ref, act_ref, xg, sem, *, seq):
    cap = xg.shape[0]
    e, b = pl.program_id(0), pl.program_id(1)
    n_e, n_b = pl.num_programs(0), pl.num_programs(1)
    step = e * n_b + b

    def gather(ir, base):
        def issue(c, carry):
            _row_copy(x_hbm, xg, base + ir[0, c], c, sem).start()
            return carry

        def drain(c, carry):
            _row_copy(x_hbm, xg, base + ir[0, c], c, sem).wait()
            return carry

        return issue, drain

    issue_cur, drain_cur = gather(idx_ref, b * seq)
    _, b_next = _near_step(e, b, n_e, n_b, 1)
    issue_next, drain_next = gather(idx_next_ref, b_next * seq)

    @pl.when(step == 0)
    def _():
        lax.fori_loop(0, cap, issue_cur, 0, unroll=DMA_LOOP_UNROLL)

    lax.fori_loop(0, cap, drain_cur, 0, unroll=DMA_LOOP_UNROLL)
    h = _rmsnorm_rows(xg[...], nw_ref[...]).astype(BF16)

    for c in range(cap):
        issue_next(c, 0)
    g = jnp.dot(h, wg_ref[...], preferred_element_type=F32)
    u = jnp.dot(h, wu_ref[...], preferred_element_type=F32)
    act_ref[...] = ((g * _sigmoid(g)) * u).astype(act_ref.dtype)

    @pl.when(step + 1 == n_e * n_b)
    def _():
        lax.fori_loop(0, cap, drain_next, 0, unroll=DMA_LOOP_UNROLL)


def _ffn1(idx, x2, nw, wg, wu, layer, seq):
    bn, ne, cap = idx.shape
    d = x2.shape[1]
    ff = wg.shape[-1]
    idx4 = idx.reshape(bn, ne, 1, cap)
    return pl.pallas_call(
        functools.partial(_ffn1_kernel, seq=seq),
        grid=(ne, bn),
        in_specs=[
            _idx_spec(ne, bn, cap, 0), _idx_spec(ne, bn, cap, 1),
            pl.BlockSpec(memory_space=pl.ANY),
            pl.BlockSpec((1, d), lambda e, b: (0, 0)),
            pl.BlockSpec((None, None, d, ff), lambda e, b: (layer, e, 0, 0)),
            pl.BlockSpec((None, None, d, ff), lambda e, b: (layer, e, 0, 0)),
        ],
        out_specs=pl.BlockSpec((None, None, cap, ff), lambda e, b: (e, b, 0, 0)),
        out_shape=jax.ShapeDtypeStruct((ne, bn, cap, ff), BF16),
        scratch_shapes=[pltpu.VMEM((cap, d), F32), pltpu.SemaphoreType.DMA(())],
        compiler_params=_cparams("arbitrary", "arbitrary"),
        name="moe_ffn1",
    )(idx4, idx4, x2, nw.reshape(1, d), wg, wu)


def _ffn2_kernel(idx_ref, idx_next_ref, idx_prev_ref, gate_ref, act_ref, wd_ref, x_in, x_out,
                 xo, sem_in, sem_out, *, seq):
    del x_in
    cap = xo.shape[1]
    e, b = pl.program_id(0), pl.program_id(1)
    n_e, n_b = pl.num_programs(0), pl.num_programs(1)
    step = e * n_b + b
    slot = step % 2

    def loop(fn):
        lax.fori_loop(0, cap, fn, 0, unroll=DMA_LOOP_UNROLL)

    def gather(ir, base, sl):
        def copy(c):
            return _row_copy(x_out, xo.at[sl], base + ir[0, c], c, sem_in.at[sl])

        def issue(c, carry):
            copy(c).start()
            return carry

        def drain(c, carry):
            copy(c).wait()
            return carry

        return issue, drain

    def scatter(ir, base, sl):
        def copy(c):
            return _row_copy(xo.at[sl], x_out, c, base + ir[0, c], sem_out.at[sl])

        def issue(c, carry):
            copy(c).start()
            return carry

        def drain(c, carry):
            copy(c).wait()
            return carry

        return issue, drain

    issue_cur, drain_cur = gather(idx_ref, b * seq, slot)
    _, b_next = _near_step(e, b, n_e, n_b, 1)
    issue_next, drain_next = gather(idx_next_ref, b_next * seq, 1 - slot)
    issue_put, drain_put = scatter(idx_ref, b * seq, slot)

    @pl.when(step == 0)
    def _():
        loop(issue_cur)

    @pl.when(step > 0)
    def _():
        _, b_prev = _near_step(e, b, n_e, n_b, -1)
        loop(scatter(idx_prev_ref, b_prev * seq, 1 - slot)[1])

    for c in range(cap):
        issue_next(c, 0)
    ye = jnp.dot(act_ref[...], wd_ref[...], preferred_element_type=F32) * gate_ref[...]

    loop(drain_cur)
    xo[slot] = xo[slot] + ye
    for c in range(cap):
        issue_put(c, 0)

    @pl.when(step + 1 == n_e * n_b)
    def _():
        loop(drain_put)
        loop(drain_next)


def _ffn2(idx, gate, act, wd, layer, x2, seq):
    bn, ne, cap = idx.shape
    assert bn >= 2
    t, d = x2.shape
    ff = wd.shape[2]
    idx4 = idx.reshape(bn, ne, 1, cap)
    return pl.pallas_call(
        functools.partial(_ffn2_kernel, seq=seq),
        grid=(ne, bn),
        in_specs=[
            _idx_spec(ne, bn, cap, 0), _idx_spec(ne, bn, cap, 1), _idx_spec(ne, bn, cap, -1),
            pl.BlockSpec((None, None, cap, 1), lambda e, b: (b, e, 0, 0)),
            pl.BlockSpec((None, None, cap, ff), lambda e, b: (e, b, 0, 0)),
            pl.BlockSpec((None, None, ff, d), lambda e, b: (layer, e, 0, 0)),
            pl.BlockSpec(memory_space=pl.ANY),
        ],
        out_specs=pl.BlockSpec(memory_space=pl.ANY),
        out_shape=jax.ShapeDtypeStruct((t, d), F32),
        input_output_aliases={6: 0},
        scratch_shapes=[pltpu.VMEM((2, cap, d), F32), pltpu.SemaphoreType.DMA((2,)),
                        pltpu.SemaphoreType.DMA((2,))],
        compiler_params=_cparams("arbitrary", "arbitrary"),
        name="moe_ffn2",
    )(idx4, idx4, idx4, gate, act, wd, x2)


def kernel(x, mem, positions, w_in, w_out, norm_mix_w, norm_cross_w, norm_mem_w, norm_ffn_w, final_norm_w, ret_decay, ret_gn_w, s5_lam_re, s5_lam_im, s5_log_dt, s5_b_re, s5_b_im, s5_c_re, s5_c_im, s5_d, s5_glu_w, cross_wq, cross_wkv, cross_wo, router_w, expert_w_gate, expert_w_up, expert_w_down):
    bn, s, d = x.shape
    depth = w_in.shape[0]
    n_mem = mem.shape[1]
    width_a = 3 * d // 8
    width_b = 3 * d // 8
    width_c = d - width_a - width_b
    n_heads_a = width_a // HEAD_DIM_A
    q_off_b = 3 * width_a
    u_off = q_off_b + 2 * N_HEADS_B * LANES + 2 * width_b
    capacity = EC_CAPACITY_FACTOR * s // N_EXPERTS
    t = bn * s

    ca, sa, cbt, sbt = _rope_tables(positions)
    x2 = x.reshape(t, d)
    mem2 = mem.reshape(bn * n_mem, d)
    w_gate_b = expert_w_gate.astype(BF16)
    w_up_b = expert_w_up.astype(BF16)
    w_down_b = expert_w_down.astype(BF16)
    for l in range(depth):
        h = _rmsnorm(x2, norm_mix_w[l], BF16)
        proj = _matmul(h, w_in, l, F32)
        proj3 = proj.reshape(bn, s, -1)
        a_out = _attn_a(proj3, ca, sa, n_heads_a)
        b_out = _retention(proj3, cbt, sbt, ret_decay[l], ret_gn_w[l], q_off_b, N_HEADS_B)
        s5_tabs = _s5_tables(s5_lam_re[l], s5_lam_im[l], s5_log_dt[l], s5_b_re[l], s5_b_im[l],
                             s5_c_re[l], s5_c_im[l])
        ys = _s5(proj3[:, :, u_off:], s5_tabs)
        c_out = _s5_glu(ys, proj, u_off, s5_d[l], s5_glu_w[l].astype(BF16))
        x2 = _out_proj(a_out.reshape(t, width_a), b_out.reshape(t, width_b), c_out, w_out, l, x2)

        mn = _rmsnorm(mem2, norm_mem_w[l], BF16)
        kv = _matmul(mn, cross_wkv, l, BF16)
        a_fold, vw_fold = _cross_fold(kv.reshape(bn, n_mem, 2 * d), cross_wq, cross_wo, l, N_HEADS_X)
        x3 = x2.reshape(bn, s, d)
        probs = _cross_probs(x3, norm_cross_w[l], a_fold, N_HEADS_X)
        x2 = _bmm_res(probs, vw_fold, x3).reshape(t, d)

        logits = _router_logits(x2, norm_ffn_w[l], router_w[l].T.astype(BF16))
        idx, gate = _topk(logits, bn, capacity)
        act = _ffn1(idx, x2, norm_ffn_w[l], w_gate_b, w_up_b, l, s)
        x2 = _ffn2(idx, gate, act, w_down_b, l, x2, s)
    return _rmsnorm(x2, final_norm_w, x.dtype).reshape(bn, s, d)
```

```python
import functools
import math

import numpy as np
import jax
import jax.numpy as jnp
from jax import lax
from jax.experimental import pallas as pl
from jax.experimental.pallas import tpu as pltpu

F32 = jnp.float32
BF16 = jnp.bfloat16

EPS = 1e-6
NEG_INF = -1e30
VMEM_LIMIT_BYTES = 56 * 1024 * 1024
LANES = 128

HEAD_DIM_A = 128
ROT_DIM_A = HEAD_DIM_A // 4
ROPE_THETA = 500000.0
DILATION_PAIRS = ((128, 1), (512, 4), (2048, 16))
ATT_BLK = 128
N_HEADS_B = 6
RET_CHUNK = 128
RET_THETA = 10000.0
S5_GROUP = 16
S5_STATE = 64
S5_CHUNK = 32
N_HEADS_X = 4
N_EXPERTS = 16
EC_CAPACITY_FACTOR = 2


def _cparams(*sem):
    return pltpu.CompilerParams(dimension_semantics=sem, vmem_limit_bytes=VMEM_LIMIT_BYTES)


def _sigmoid(x):
    return 1.0 / (1.0 + jnp.exp(-x))


def _rmsnorm_rows(x, w):
    ms = jnp.mean(x * x, axis=-1, keepdims=True)
    return x * lax.rsqrt(ms + EPS) * w


def _rmsnorm_kernel(x_ref, w_ref, o_ref):
    o_ref[...] = _rmsnorm_rows(x_ref[...], w_ref[...]).astype(o_ref.dtype)


def _rmsnorm(x2, w, out_dtype, tm=512):
    t, d = x2.shape
    tm = min(tm, t)
    return pl.pallas_call(
        _rmsnorm_kernel,
        grid=(t // tm,),
        in_specs=[pl.BlockSpec((tm, d), lambda i: (i, 0)), pl.BlockSpec((1, d), lambda i: (0, 0))],
        out_specs=pl.BlockSpec((tm, d), lambda i: (i, 0)),
        out_shape=jax.ShapeDtypeStruct((t, d), out_dtype),
        compiler_params=_cparams("parallel"),
        name="rmsnorm",
    )(x2, w.reshape(1, d))


def _router_kernel(x_ref, w_ref, rw_ref, lg_ref):
    h = _rmsnorm_rows(x_ref[...], w_ref[...]).astype(BF16)
    lg_ref[...] = lax.dot_general(rw_ref[...], h, (((1,), (1,)), ((), ())), preferred_element_type=F32)


def _router_logits(x2, w, rw_t, tm=512):
    t, d = x2.shape
    ne = rw_t.shape[0]
    return pl.pallas_call(
        _router_kernel,
        grid=(t // tm,),
        in_specs=[
            pl.BlockSpec((tm, d), lambda i: (i, 0)),
            pl.BlockSpec((1, d), lambda i: (0, 0)),
            pl.BlockSpec((ne, d), lambda i: (0, 0)),
        ],
        out_specs=pl.BlockSpec((ne, tm), lambda i: (0, i)),
        out_shape=jax.ShapeDtypeStruct((ne, t), F32),
        compiler_params=_cparams("parallel"),
        name="router_logits",
    )(x2, w.reshape(1, d), rw_t)


def _mm_ws_kernel(a_ref, w_ref, o_ref, wb):
    @pl.when(pl.program_id(1) == 0)
    def _():
        wb[...] = w_ref[...].astype(BF16)

    o_ref[...] = jnp.dot(a_ref[...], wb[...], preferred_element_type=F32).astype(o_ref.dtype)


def _mm3_ws_kernel(a1_ref, a2_ref, a3_ref, w_ref, r_ref, o_ref, wb):
    @pl.when(pl.program_id(1) == 0)
    def _():
        wb[...] = w_ref[...].astype(BF16)

    k1 = a1_ref.shape[1]
    k2 = a2_ref.shape[1]
    acc = jnp.dot(a1_ref[...], wb[0:k1, :], preferred_element_type=F32)
    acc += jnp.dot(a2_ref[...], wb[k1:k1 + k2, :], preferred_element_type=F32)
    acc += jnp.dot(a3_ref[...], wb[k1 + k2:, :], preferred_element_type=F32)
    o_ref[...] = r_ref[...] + acc


def _matmul(a, w3, layer, out_dtype, tm=1024, tn=512):
    m, k = a.shape
    n = w3.shape[2]
    tm = min(tm, m)
    tn = min(tn, n)
    return pl.pallas_call(
        _mm_ws_kernel,
        grid=(n // tn, m // tm),
        in_specs=[pl.BlockSpec((tm, k), lambda j, i: (i, 0)),
                  pl.BlockSpec((None, k, tn), lambda j, i: (layer, 0, j))],
        out_specs=pl.BlockSpec((tm, tn), lambda j, i: (i, j)),
        out_shape=jax.ShapeDtypeStruct((m, n), out_dtype),
        scratch_shapes=[pltpu.VMEM((k, tn), BF16)],
        compiler_params=_cparams("parallel", "arbitrary"),
        name="matmul",
    )(a, w3)


def _out_proj(a1, a2, a3, w3, layer, res, tm=1024, tn=512):
    m = a1.shape[0]
    k1, k2, k3 = a1.shape[1], a2.shape[1], a3.shape[1]
    k, n = w3.shape[1], w3.shape[2]
    assert k1 + k2 + k3 == k
    return pl.pallas_call(
        _mm3_ws_kernel,
        grid=(n // tn, m // tm),
        in_specs=[
            pl.BlockSpec((tm, k1), lambda j, i: (i, 0)),
            pl.BlockSpec((tm, k2), lambda j, i: (i, 0)),
            pl.BlockSpec((tm, k3), lambda j, i: (i, 0)),
            pl.BlockSpec((None, k, tn), lambda j, i: (layer, 0, j)),
            pl.BlockSpec((tm, tn), lambda j, i: (i, j)),
        ],
        out_specs=pl.BlockSpec((tm, tn), lambda j, i: (i, j)),
        out_shape=jax.ShapeDtypeStruct((m, n), F32),
        scratch_shapes=[pltpu.VMEM((k, tn), BF16)],
        compiler_params=_cparams("parallel", "arbitrary"),
        name="out_proj",
    )(a1, a2, a3, w3, res)


def _rope_kernel(pos_ref, fa_ref, fb_ref, ca_ref, sa_ref, cb_ref, sb_ref):
    pos = pos_ref[...].astype(F32)
    ang_a = pos * fa_ref[...]
    ca_ref[...] = jnp.cos(ang_a)
    sa_ref[...] = jnp.sin(ang_a)
    ang_b = pos * fb_ref[...]
    lane = lax.broadcasted_iota(jnp.int32, ang_b.shape, 1)
    sin_b = jnp.sin(ang_b)
    cb_ref[...] = jnp.cos(ang_b)
    sb_ref[...] = jnp.where(lane < LANES // 2, -sin_b, sin_b)


def _rope_tables(positions):
    b, s = positions.shape
    half_a = ROT_DIM_A // 2
    fa = ROPE_THETA ** (-(jnp.arange(half_a, dtype=F32) * 2.0 / ROT_DIM_A))
    fa_row = jnp.concatenate([fa, fa, jnp.zeros((LANES - 2 * half_a,), F32)]).reshape(1, LANES)
    fb = RET_THETA ** (-jnp.linspace(0.0, 1.0, LANES // 2, dtype=F32))
    fb_row = jnp.concatenate([fb, fb]).reshape(1, LANES)
    tab = jax.ShapeDtypeStruct((b, s, LANES), F32)
    row = pl.BlockSpec((1, LANES), lambda i: (0, 0))
    out = pl.BlockSpec((None, s, LANES), lambda i: (i, 0, 0))
    return pl.pallas_call(
        _rope_kernel,
        grid=(b,),
        in_specs=[pl.BlockSpec((None, s, 1), lambda i: (i, 0, 0)), row, row],
        out_specs=[out, out, out, out],
        out_shape=[tab, tab, tab, tab],
        compiler_params=_cparams("parallel"),
        name="rope_tables",
    )(positions.reshape(b, s, 1), fa_row, fb_row)


def _attn_a_kernel(q_ref, k_ref, v_ref, c_ref, s_ref, o_ref, qn, kn, accn, mn, ln, *, seq, radius):
    rc = 128
    scale = HEAD_DIM_A ** -0.5
    half = ROT_DIM_A // 2

    def rot_body(i, carry):
        rows = pl.ds(pl.multiple_of(i * rc, rc), rc)
        lane = lax.broadcasted_iota(jnp.int32, (rc, LANES), 1)
        c = c_ref[rows, :]
        s = s_ref[rows, :]
        s_lo = jnp.where(lane < half, -s, 0.0)
        s_hi = jnp.where(lane >= half, s, 0.0)
        q = q_ref[rows, :]
        k = k_ref[rows, :]
        qn[rows, :] = (q * c + pltpu.roll(q, LANES - half, 1) * s_lo + pltpu.roll(q, half, 1) * s_hi) * scale
        kn[rows, :] = k * c + pltpu.roll(k, LANES - half, 1) * s_lo + pltpu.roll(k, half, 1) * s_hi
        return carry

    lax.fori_loop(0, seq // rc, rot_body, 0, unroll=2)

    for branch, (_, dil) in enumerate(DILATION_PAIRS):
        sub = seq // dil
        blk = sub if sub <= 2 * ATT_BLK else ATT_BLK
        win = min(sub, blk + 2 * radius)
        nbr = sub // blk
        dmat = (lax.broadcasted_iota(jnp.int32, (blk, win), 1)
                - lax.broadcasted_iota(jnp.int32, (blk, win), 0))

        def rows_of(start, size, dil=dil):
            return pl.ds(start, size) if dil == 1 else pl.ds(start, size, stride=dil)

        def blk_body(n, carry, branch=branch, dil=dil, sub=sub, nbr=nbr, rows_of=rows_of,
                     blk=blk, win=win, dmat=dmat):
            r = n // nbr
            j = n - r * nbr
            wrel = jnp.clip(j * blk - radius, 0, sub - win)
            q_rows = rows_of(r + dil * (j * blk), blk)
            k_rows = rows_of(r + dil * wrel, win)
            qb = qn[q_rows, :].astype(BF16)
            kw = kn[k_rows, :].astype(BF16)
            vw = v_ref[k_rows, :].astype(BF16)
            sc = lax.dot_general(qb, kw, (((1,), (1,)), ((), ())), preferred_element_type=F32)
            sc = jnp.where(jnp.abs(dmat + (wrel - j * blk)) <= radius, sc, NEG_INF)
            m = jnp.max(sc, axis=1, keepdims=True)
            p = jnp.exp(sc - m)
            l = jnp.broadcast_to(jnp.sum(p, axis=1, keepdims=True), (blk, LANES))
            acc = jnp.dot(p.astype(BF16), vw, preferred_element_type=F32)
            m = jnp.broadcast_to(m, (blk, LANES))
            if branch > 0:
                m_old = mn[q_rows, :]
                m_max = jnp.maximum(m_old, m)
                e_old = jnp.exp(m_old - m_max)
                e_new = jnp.exp(m - m_max)
                acc = accn[q_rows, :] * e_old + acc * e_new
                l = ln[q_rows, :] * e_old + l * e_new
                m = m_max
            accn[q_rows, :] = acc
            mn[q_rows, :] = m
            ln[q_rows, :] = l
            return carry

        lax.fori_loop(0, seq // blk, blk_body, 0, unroll=8 if blk == ATT_BLK else 4)

    def out_body(i, carry):
        rows = pl.ds(pl.multiple_of(i * rc, rc), rc)
        o_ref[rows, :] = (accn[rows, :] / ln[rows, :]).astype(o_ref.dtype)
        return carry

    lax.fori_loop(0, seq // rc, out_body, 0, unroll=4)


def _attn_a(proj3, ca, sa, n_heads):
    b, s, _ = proj3.shape
    radii = {w // (2 * d) for w, d in DILATION_PAIRS}
    assert len(radii) == 1
    radius = radii.pop()
    for _, d in DILATION_PAIRS:
        assert s % (d * ATT_BLK) == 0 and s // d >= 2 * ATT_BLK

    def col(off):
        return pl.BlockSpec((None, s, HEAD_DIM_A), lambda i, h: (i, 0, off + h))

    tab = pl.BlockSpec((None, s, LANES), lambda i, h: (i, 0, 0))
    f32s = lambda: pltpu.VMEM((s, LANES), F32)
    return pl.pallas_call(
        functools.partial(_attn_a_kernel, seq=s, radius=radius),
        grid=(b, n_heads),
        in_specs=[col(0), col(n_heads), col(2 * n_heads), tab, tab],
        out_specs=pl.BlockSpec((None, s, HEAD_DIM_A), lambda i, h: (i, 0, h)),
        out_shape=jax.ShapeDtypeStruct((b, s, n_heads * HEAD_DIM_A), BF16),
        scratch_shapes=[f32s(), f32s(), f32s(), f32s(), f32s()],
        compiler_params=_cparams("parallel", "parallel"),
        name="attn_a",
    )(proj3, proj3, proj3, ca, sa)


def _ret_kernel(q_ref, k_ref, v_ref, g_ref, c_ref, s_ref, dm_ref, dec_ref, cdec_ref, gn_ref, o_ref,
                qn, kn, racc, sf, sb, *, seq):
    cb = RET_CHUNK
    nc = seq // cb
    dk = q_ref.shape[-1]
    dv = v_ref.shape[-1]
    h = pl.program_id(1)
    kscale = dk ** -0.5

    def rot_body(i, carry):
        rows = pl.ds(pl.multiple_of(i * cb, cb), cb)
        c = c_ref[rows, :]
        s = s_ref[rows, :]
        q = q_ref[rows, :]
        k = k_ref[rows, :]
        qn[rows, :] = q * c + pltpu.roll(q, dk // 2, 1) * s
        kn[rows, :] = (k * c + pltpu.roll(k, dk // 2, 1) * s) * kscale
        racc[rows, :] = jnp.zeros((cb, dv), F32)
        return carry

    lax.fori_loop(0, nc, rot_body, 0, unroll=2)
    sf[...] = jnp.zeros(sf.shape, F32)
    sb[...] = jnp.zeros(sb.shape, F32)
    c_f = cdec_ref[h, 0]
    c_b = cdec_ref[h, 1]

    def body(n, carry):
        rows_f = pl.ds(pl.multiple_of(n * cb, cb), cb)
        rows_b = pl.ds(pl.multiple_of((nc - 1 - n) * cb, cb), cb)
        q = qn[rows_f, :]
        k = kn[rows_f, :]
        v = v_ref[rows_f, :].astype(BF16)
        sc = lax.dot_general(q.astype(BF16), k.astype(BF16), (((1,), (1,)), ((), ())),
                             preferred_element_type=F32) * dm_ref[...]
        intra = jnp.dot(sc.astype(BF16), v, preferred_element_type=F32)
        cross = jnp.dot((q * dec_ref[0]).astype(BF16), sf[...].astype(BF16), preferred_element_type=F32)
        racc[rows_f, :] += intra + cross
        kd = (k * dec_ref[1]).T.astype(BF16)
        sf[...] = c_f * sf[...] + jnp.dot(kd, v, preferred_element_type=F32)
        q2 = qn[rows_b, :]
        k2 = kn[rows_b, :]
        v2 = v_ref[rows_b, :].astype(BF16)
        cross_b = jnp.dot((q2 * dec_ref[2]).astype(BF16), sb[...].astype(BF16), preferred_element_type=F32)
        racc[rows_b, :] += cross_b
        kd2 = (k2 * dec_ref[3]).T.astype(BF16)
        sb[...] = c_b * sb[...] + jnp.dot(kd2, v2, preferred_element_type=F32)
        return carry

    lax.fori_loop(0, nc, body, 0, unroll=2)

    def out_body(i, carry):
        rows = pl.ds(pl.multiple_of(i * cb, cb), cb)
        r = racc[rows, :]
        mu = jnp.mean(r, axis=-1, keepdims=True)
        xc = r - mu
        var = jnp.mean(xc * xc, axis=-1, keepdims=True)
        y = xc * lax.rsqrt(var + EPS) * gn_ref[...]
        g = g_ref[rows, :]
        o_ref[rows, :] = (y * (g * _sigmoid(g))).astype(o_ref.dtype)
        return carry

    lax.fori_loop(0, nc, out_body, 0, unroll=4)


def _ret_tables(ret_decay_l):
    cb = RET_CHUNK
    log_g = jax.nn.log_sigmoid(ret_decay_l.astype(F32))
    lf = log_g[0][:, None, None]
    lb = log_g[1][:, None, None]
    i = jnp.arange(cb)[:, None]
    j = jnp.arange(cb)[None, :]
    dfwd = jnp.where(i >= j, i - j, 0).astype(F32)[None]
    dbwd = jnp.where(j > i, j - i, 0).astype(F32)[None]
    dmask = jnp.where((i >= j)[None], jnp.exp(dfwd * lf), jnp.exp(dbwd * lb))
    pos = jnp.arange(cb, dtype=F32)[None, :]
    lf2, lb2 = log_g[0][:, None], log_g[1][:, None]
    dec = jnp.stack([jnp.exp((pos + 1.0) * lf2), jnp.exp((cb - 1.0 - pos) * lf2),
                     jnp.exp((cb - pos) * lb2), jnp.exp(pos * lb2)], axis=1)
    dec = jnp.broadcast_to(dec[..., None], dec.shape + (LANES,))
    cdec = jnp.exp(cb * log_g).T
    return dmask, dec, cdec


def _retention(proj3, cbt, sbt, ret_decay_l, gn_w, q_off, n_heads):
    b, s, _ = proj3.shape
    dk = LANES
    dv = 2 * dk
    dmask, dec, cdec = _ret_tables(ret_decay_l)
    qb0 = q_off // dk
    kb0 = qb0 + n_heads
    vb0 = (q_off + 2 * n_heads * dk) // dv
    gb0 = vb0 + n_heads
    tab = pl.BlockSpec((None, s, LANES), lambda i, h: (i, 0, 0))
    return pl.pallas_call(
        functools.partial(_ret_kernel, seq=s),
        grid=(b, n_heads),
        in_specs=[
            pl.BlockSpec((None, s, dk), lambda i, h: (i, 0, qb0 + h)),
            pl.BlockSpec((None, s, dk), lambda i, h: (i, 0, kb0 + h)),
            pl.BlockSpec((None, s, dv), lambda i, h: (i, 0, vb0 + h)),
            pl.BlockSpec((None, s, dv), lambda i, h: (i, 0, gb0 + h)),
            tab, tab,
            pl.BlockSpec((None, RET_CHUNK, RET_CHUNK), lambda i, h: (h, 0, 0)),
            pl.BlockSpec((None, 4, RET_CHUNK, LANES), lambda i, h: (h, 0, 0, 0)),
            pl.BlockSpec(memory_space=pltpu.SMEM),
            pl.BlockSpec((None, 1, dv), lambda i, h: (h, 0, 0)),
        ],
        out_specs=pl.BlockSpec((None, s, dv), lambda i, h: (i, 0, h)),
        out_shape=jax.ShapeDtypeStruct((b, s, n_heads * dv), BF16),
        scratch_shapes=[pltpu.VMEM((s, dk), F32), pltpu.VMEM((s, dk), F32), pltpu.VMEM((s, dv), F32),
                        pltpu.VMEM((dk, dv), F32), pltpu.VMEM((dk, dv), F32)],
        compiler_params=_cparams("parallel", "parallel"),
        name="retention",
    )(proj3, proj3, proj3, proj3, cbt, sbt, dmask, dec, cdec, gn_w.reshape(n_heads, 1, dv))


def _split_dot_nt(a, b):
    dims = (((1,), (1,)), ((), ()))
    a_hi = a.astype(BF16)
    a_lo = (a - a_hi.astype(F32)).astype(BF16)
    b_hi = b.astype(BF16)
    b_lo = (b - b_hi.astype(F32)).astype(BF16)
    out = lax.dot_general(a_hi, b_hi, dims, preferred_element_type=F32)
    out += lax.dot_general(a_hi, b_lo, dims, preferred_element_type=F32)
    out += lax.dot_general(a_lo, b_hi, dims, preferred_element_type=F32)
    return out


def _s5_kernel(u_ref, pw_ref, bb_ref, cc_ref, a_ref, y_ref, kv, hp, m_sc, e_sc, ft_sc, ca_sc, *, n_chunks, bn):
    p = S5_STATE
    c = S5_GROUP
    L = S5_CHUNK
    bbr, bbi = bb_ref[0], bb_ref[1]
    ccr, cci = cc_ref[0], cc_ref[1]
    for j in range(L):
        rows = pl.ds(j * c, c)
        pr, pi = pw_ref[0, j:j + 1, 0:2 * p], pw_ref[0, j:j + 1, 2 * p:4 * p]
        e_sc[rows, 0:2 * p] = (pr * bbr - pi * bbi).astype(BF16)
        e_sc[rows, 2 * p:4 * p] = (pr * bbi + pi * bbr).astype(BF16)
        pr, pi = pw_ref[1, j:j + 1, 0:2 * p], pw_ref[1, j:j + 1, 2 * p:4 * p]
        ft_sc[rows, 0:2 * p] = (pr * ccr - pi * cci).astype(BF16)
        ft_sc[rows, 2 * p:4 * p] = (-(pr * cci + pi * ccr)).astype(BF16)
    for s in range(2 * L):
        rows = pl.ds(s * c, c)
        pr, pi = pw_ref[2, s:s + 1, 0:2 * p], pw_ref[2, s:s + 1, 2 * p:4 * p]
        ca_sc[rows, 0:2 * p] = pr * ccr - pi * cci
        ca_sc[rows, 2 * p:4 * p] = pr * cci + pi * ccr
    strip = _split_dot_nt(jnp.concatenate([bbr, -bbi], axis=1), ca_sc[...])
    for j in range(L):
        m_sc[j * c:(j + 1) * c, :] = strip[:, (L - 1 - j) * c:(2 * L - 1 - j) * c].astype(BF16)
    u = u_ref[...]
    kv[...] = jnp.dot(u, e_sc[...], preferred_element_type=F32)
    a_re = a_ref[0:1, :]
    a_im = a_ref[1:2, :]
    is_fwd = lax.broadcasted_iota(jnp.int32, (bn, 2 * p), 1) < p
    h_re = jnp.zeros((bn, 2 * p), F32)
    h_im = jnp.zeros((bn, 2 * p), F32)
    for n in range(n_chunks):
        rf = pl.ds(n * bn, bn)
        rb = pl.ds((n_chunks - 1 - n) * bn, bn)
        hp[rf, 0:p] = h_re[:, 0:p]
        hp[rb, p:2 * p] = h_re[:, p:2 * p]
        hp[rf, 2 * p:3 * p] = h_im[:, 0:p]
        hp[rb, 3 * p:4 * p] = h_im[:, p:2 * p]
        kv_re = jnp.where(is_fwd, kv[rf, 0:2 * p], kv[rb, 0:2 * p])
        kv_im = jnp.where(is_fwd, kv[rf, 2 * p:4 * p], kv[rb, 2 * p:4 * p])
        h_re, h_im = (a_re * h_re - a_im * h_im + kv_re, a_re * h_im + a_im * h_re + kv_im)
    y = jnp.dot(u, m_sc[...], preferred_element_type=F32)
    y_ref[...] = y + lax.dot_general(hp[...].astype(BF16), ft_sc[...], (((1,), (1,)), ((), ())),
                                     preferred_element_type=F32)


def _s5_tables(lam_re, lam_im, log_dt, b_re, b_im, c_re, c_im):
    L = S5_CHUNK
    lam_re = lam_re.astype(F32)
    lam_im = lam_im.astype(F32)
    dt = jnp.exp(log_dt.astype(F32))[..., None]
    mag = jnp.exp(lam_re * dt)
    ar = mag * jnp.cos(lam_im * dt)
    ai = mag * jnp.sin(lam_im * dt)
    den = lam_re * lam_re + lam_im * lam_im
    nr = ar - 1.0
    cr = (nr * lam_re + ai * lam_im) / den
    ci = (ai * lam_re - nr * lam_im) / den
    b_re = b_re.astype(F32)
    b_im = b_im.astype(F32)
    bbr = cr[..., None] * b_re - ci[..., None] * b_im
    bbi = cr[..., None] * b_im + ci[..., None] * b_re

    def cat(x):
        return jnp.concatenate([x[0], x[1]], axis=-1)

    bb = jnp.stack([cat(bbr.transpose(0, 1, 3, 2)), cat(bbi.transpose(0, 1, 3, 2))], axis=1)
    cc = jnp.stack([cat(c_re.astype(F32)), cat(c_im.astype(F32))], axis=1)
    tau = jnp.arange(L + 1, dtype=F32)[:, None, None, None]
    pmag = jnp.exp(tau * (lam_re * dt)[None])
    pang = tau * (lam_im * dt)[None]
    pr = pmag * jnp.cos(pang)
    pi = pmag * jnp.sin(pang)

    def rows(f_idx, b_idx, f_ok=None, b_ok=None):
        f_r, f_i, b_r, b_i = pr[f_idx, 0], pi[f_idx, 0], pr[b_idx, 1], pi[b_idx, 1]
        if f_ok is not None:
            f_r, f_i = f_r * f_ok, f_i * f_ok
            b_r, b_i = b_r * b_ok, b_i * b_ok
        return jnp.concatenate([f_r, b_r, f_i, b_i], axis=-1).transpose(1, 0, 2)

    k = jnp.arange(L)
    pad = jnp.zeros((lam_re.shape[1], L, 4 * S5_STATE), F32)
    pw_e = jnp.concatenate([rows(L - 1 - k, k), pad], axis=1)
    pw_f = jnp.concatenate([rows(k + 1, L - k), pad], axis=1)
    lag = jnp.arange(2 * L) - (L - 1)
    f_ok = ((lag >= 0) & (lag < L)).astype(F32)[:, None, None]
    b_ok = (lag <= 0).astype(F32)[:, None, None]
    pw_s = rows(jnp.clip(lag, 0, L), jnp.clip(-lag, 0, L), f_ok, b_ok)
    pw = jnp.stack([pw_e, pw_f, pw_s], axis=1)
    a_l = jnp.stack([jnp.concatenate([pr[L, 0], pr[L, 1]], axis=-1),
                     jnp.concatenate([pi[L, 0], pi[L, 1]], axis=-1)], axis=1)
    return pw, bb, cc, a_l


def _s5(uc3, tables):
    pw, bb, cc, a_l = tables
    b, s, w = uc3.shape
    c = S5_GROUP
    g = w // c
    L = S5_CHUNK
    p = S5_STATE
    nc = s // L
    rows = nc * b
    ug = uc3.reshape(b, nc, L, g, c).transpose(3, 1, 0, 2, 4).reshape(g, rows, L * c).astype(BF16)
    ys = pl.pallas_call(
        functools.partial(_s5_kernel, n_chunks=nc, bn=b),
        grid=(g,),
        in_specs=[
            pl.BlockSpec((None, rows, L * c), lambda i: (i, 0, 0)),
            pl.BlockSpec((None, 3, 2 * L, 4 * p), lambda i: (i, 0, 0, 0)),
            pl.BlockSpec((None, 2, c, 2 * p), lambda i: (i, 0, 0, 0)),
            pl.BlockSpec((None, 2, c, 2 * p), lambda i: (i, 0, 0, 0)),
            pl.BlockSpec((None, 2, 2 * p), lambda i: (i, 0, 0)),
        ],
        out_specs=pl.BlockSpec((None, rows, L * c), lambda i: (i, 0, 0)),
        out_shape=jax.ShapeDtypeStruct((g, rows, L * c), F32),
        scratch_shapes=[pltpu.VMEM((rows, 4 * p), F32), pltpu.VMEM((rows, 4 * p), F32),
                        pltpu.VMEM((L * c, L * c), BF16), pltpu.VMEM((L * c, 4 * p), BF16),
                        pltpu.VMEM((L * c, 4 * p), BF16), pltpu.VMEM((2 * L * c, 4 * p), F32)],
        compiler_params=_cparams("parallel"),
        name="s5",
    )(ug, pw, bb, cc, a_l)
    return ys.reshape(g, nc, b, L, c).transpose(2, 1, 3, 0, 4).reshape(b * s, w)


def _glu_kernel(ys_ref, u_ref, d_ref, w_ref, o_ref):
    y = ys_ref[...] + u_ref[...] * d_ref[...]
    y = 0.5 * y * (1.0 + jnp.tanh(math.sqrt(2.0 / math.pi) * (y + 0.044715 * (y * y * y))))
    z = jnp.dot(y.astype(BF16), w_ref[...], preferred_element_type=F32)
    o_ref[...] = (y * _sigmoid(z)).astype(o_ref.dtype)


def _s5_glu(ys, proj2, u_off, d_w, glu_w, tm=512):
    t, w = ys.shape
    ub = u_off // w
    return pl.pallas_call(
        _glu_kernel,
        grid=(t // tm,),
        in_specs=[
            pl.BlockSpec((tm, w), lambda i: (i, 0)),
            pl.BlockSpec((tm, w), lambda i: (i, ub)),
            pl.BlockSpec((1, w), lambda i: (0, 0)),
            pl.BlockSpec((w, w), lambda i: (0, 0)),
        ],
        out_specs=pl.BlockSpec((tm, w), lambda i: (i, 0)),
        out_shape=jax.ShapeDtypeStruct((t, w), BF16),
        compiler_params=_cparams("parallel"),
        name="s5_glu",
    )(ys, proj2, d_w.reshape(1, w), glu_w)


def _cross_qk_kernel(wq_ref, km_ref, a_ref, wb):
    @pl.when(pl.program_id(1) == 0)
    def _():
        wb[...] = wq_ref[...].astype(BF16)

    a_ref[...] = lax.dot_general(wb[...], km_ref[...], (((1,), (1,)), ((), ())),
                                 preferred_element_type=F32).astype(a_ref.dtype)


def _cross_vo_kernel(vm_ref, wo_ref, vw_ref, wb):
    @pl.when(pl.program_id(1) == 0)
    def _():
        wb[...] = wo_ref[...].astype(BF16)

    vw_ref[...] = jnp.dot(vm_ref[...], wb[...], preferred_element_type=F32).astype(vw_ref.dtype)


def _cross_fold(kv3, wq3, wo3, layer, n_heads):
    b, nm, d2 = kv3.shape
    d = d2 // 2
    hd = d // n_heads
    a = pl.pallas_call(
        _cross_qk_kernel,
        grid=(n_heads, b),
        in_specs=[pl.BlockSpec((None, d, hd), lambda h, i: (layer, 0, h)),
                  pl.BlockSpec((None, nm, hd), lambda h, i: (i, 0, h))],
        out_specs=pl.BlockSpec((None, d, nm), lambda h, i: (i, 0, h)),
        out_shape=jax.ShapeDtypeStruct((b, d, n_heads * nm), BF16),
        scratch_shapes=[pltpu.VMEM((d, hd), BF16)],
        compiler_params=_cparams("parallel", "arbitrary"),
        name="cross_fold_qk",
    )(wq3, kv3)
    vw = pl.pallas_call(
        _cross_vo_kernel,
        grid=(n_heads, b),
        in_specs=[pl.BlockSpec((None, nm, hd), lambda h, i: (i, 0, n_heads + h)),
                  pl.BlockSpec((None, hd, d), lambda h, i: (layer, h, 0))],
        out_specs=pl.BlockSpec((None, nm, d), lambda h, i: (i, h, 0)),
        out_shape=jax.ShapeDtypeStruct((b, n_heads * nm, d), BF16),
        scratch_shapes=[pltpu.VMEM((hd, d), BF16)],
        compiler_params=_cparams("parallel", "arbitrary"),
        name="cross_fold_vo",
    )(kv3, wo3)
    return a, vw


def _cross_probs_kernel(x_ref, nw_ref, a_ref, p_ref, *, n_heads, scale):
    h = _rmsnorm_rows(x_ref[...], nw_ref[...]).astype(BF16)
    sc = jnp.dot(h, a_ref[...], preferred_element_type=F32) * scale
    nm = sc.shape[1] // n_heads
    for hh in range(n_heads):
        s_h = sc[:, hh * nm:(hh + 1) * nm]
        m = jnp.max(s_h, axis=-1, keepdims=True)
        e = jnp.exp(s_h - m)
        p_ref[:, hh * nm:(hh + 1) * nm] = (e / jnp.sum(e, axis=-1, keepdims=True)).astype(p_ref.dtype)


def _cross_probs(x3, nw, a, n_heads, tq=512):
    b, s, d = x3.shape
    hm = a.shape[2]
    scale = (d // n_heads) ** -0.5
    return pl.pallas_call(
        functools.partial(_cross_probs_kernel, n_heads=n_heads, scale=scale),
        grid=(b, s // tq),
        in_specs=[pl.BlockSpec((None, tq, d), lambda i, j: (i, j, 0)),
                  pl.BlockSpec((1, d), lambda i, j: (0, 0)),
                  pl.BlockSpec((None, d, hm), lambda i, j: (i, 0, 0))],
        out_specs=pl.BlockSpec((None, tq, hm), lambda i, j: (i, j, 0)),
        out_shape=jax.ShapeDtypeStruct((b, s, hm), BF16),
        compiler_params=_cparams("parallel", "parallel"),
        name="cross_probs",
    )(x3, nw.reshape(1, d), a)


def _bmm_res_kernel(a_ref, w_ref, r_ref, o_ref):
    o_ref[...] = r_ref[...] + jnp.dot(a_ref[...], w_ref[...], preferred_element_type=F32)


def _bmm_res(a3, w3, res3, tm=1024, tn=1024):
    b, s, k = a3.shape
    n = w3.shape[2]
    return pl.pallas_call(
        _bmm_res_kernel,
        grid=(b, s // tm, n // tn),
        in_specs=[pl.BlockSpec((None, tm, k), lambda i, j, c: (i, j, 0)),
                  pl.BlockSpec((None, k, tn), lambda i, j, c: (i, 0, c)),
                  pl.BlockSpec((None, tm, tn), lambda i, j, c: (i, j, c))],
        out_specs=pl.BlockSpec((None, tm, tn), lambda i, j, c: (i, j, c)),
        out_shape=jax.ShapeDtypeStruct((b, s, n), F32),
        compiler_params=_cparams("parallel", "parallel", "parallel"),
        name="cross_out",
    )(a3, w3, res3)


def _prefix_sum_lanes(x):
    n = x.shape[-1]
    lane = lax.broadcasted_iota(jnp.int32, x.shape, x.ndim - 1)
    k = 1
    while k < n:
        x = x + jnp.where(lane >= k, pltpu.roll(x, k, x.ndim - 1), 0.0)
        k *= 2
    return x


def _topk_kernel(lg_ref, idx_ref, gate_ref, *, capacity):
    lg = lg_ref[...]
    ne, s = lg.shape
    mx = jnp.max(lg, axis=0, keepdims=True)
    ex = jnp.exp(lg - mx)
    aff = ex / jnp.sum(ex, axis=0, keepdims=True)
    bits = pltpu.bitcast(aff, jnp.int32)

    def bit_body(i, thr):
        cand = thr | jnp.left_shift(jnp.int32(1), 30 - i)
        cnt = jnp.sum((bits >= cand).astype(F32), axis=1, keepdims=True)
        return jnp.where(cnt >= capacity, cand, thr)

    thr = lax.fori_loop(0, 31, bit_body, jnp.zeros((ne, 1), jnp.int32))
    gt = bits > thr
    eq = bits == thr
    n_gt = jnp.sum(gt.astype(F32), axis=1, keepdims=True)
    eq_rank = _prefix_sum_lanes(eq.astype(F32))
    sel = gt | (eq & (eq_rank <= capacity - n_gt))
    slot = _prefix_sum_lanes(sel.astype(F32)) - 1.0
    key = jnp.where(sel, slot, -1.0)
    tok = lax.broadcasted_iota(jnp.int32, (1, s), 1).astype(F32)
    lc = min(1024, s)
    for e in range(ne):
        idx_acc = jnp.zeros((capacity, 1), F32)
        gate_acc = jnp.zeros((capacity, 1), F32)
        for c0 in range(0, s, lc):
            slots = lax.broadcasted_iota(jnp.int32, (capacity, lc), 0).astype(F32)
            hit = key[e:e + 1, c0:c0 + lc] == slots
            idx_acc += jnp.sum(jnp.where(hit, tok[:, c0:c0 + lc], 0.0), axis=1, keepdims=True)
            gate_acc += jnp.sum(jnp.where(hit, aff[e:e + 1, c0:c0 + lc], 0.0), axis=1, keepdims=True)
        idx_ref[e] = idx_acc.astype(jnp.int32)
        gate_ref[e] = gate_acc


def _topk(logits, bn, capacity):
    ne, t = logits.shape
    s = t // bn
    idx, gate = pl.pallas_call(
        functools.partial(_topk_kernel, capacity=capacity),
        grid=(bn,),
        in_specs=[pl.BlockSpec((ne, s), lambda i: (0, i))],
        out_specs=[pl.BlockSpec((None, ne, capacity, 1), lambda i: (i, 0, 0, 0)),
                   pl.BlockSpec((None, ne, capacity, 1), lambda i: (i, 0, 0, 0))],
        out_shape=[jax.ShapeDtypeStruct((bn, ne, capacity, 1), jnp.int32),
                   jax.ShapeDtypeStruct((bn, ne, capacity, 1), F32)],
        compiler_params=_cparams("parallel"),
        name="topk",
    )(logits)
    return idx.reshape(bn, ne, capacity), gate


def _row_copy(src, dst, src_row, dst_row, sem):
    return pltpu.make_async_copy(src.at[pl.ds(src_row, 1), :], dst.at[pl.ds(dst_row, 1), :], sem)


DMA_LOOP_UNROLL = 8


def _near_step(e, b, n_e, n_b, delta):
    lin = (e * n_b + b + delta + n_e * n_b) % (n_e * n_b)
    return lin // n_b, lin % n_b


def _idx_spec(n_e, n_b, cap, delta):
    def index_map(e, b):
        e2, b2 = _near_step(e, b, n_e, n_b, delta)
        return (b2, e2, 0, 0)

    return pl.BlockSpec((None, None, 1, cap), index_map, memory_space=pltpu.SMEM)


def _ffn1_kernel(idx_ref, idx_next_ref, x_hbm, nw_ref, wg_ref, wu_ref, act_ref, xg, sem, *, seq):
    cap = xg.shape[0]
    e, b = pl.program_id(0), pl.program_id(1)
    n_e, n_b = pl.num_programs(0), pl.num_programs(1)
    step = e * n_b + b

    def gather(ir, base):
        def issue(c, carry):
            _row_copy(x_hbm, xg, base + ir[0, c], c, sem).start()
            return carry

        def drain(c, carry):
            _row_copy(x_hbm, xg, base + ir[0, c], c, sem).wait()
            return carry

        return issue, drain

    issue_cur, drain_cur = gather(idx_ref, b * seq)
    _, b_next = _near_step(e, b, n_e, n_b, 1)
    issue_next, drain_next = gather(idx_next_ref, b_next * seq)

    @pl.when(step == 0)
    def _():
        lax.fori_loop(0, cap, issue_cur, 0, unroll=DMA_LOOP_UNROLL)

    lax.fori_loop(0, cap, drain_cur, 0, unroll=DMA_LOOP_UNROLL)
    h = _rmsnorm_rows(xg[...], nw_ref[...]).astype(BF16)

    for c in range(cap):
        issue_next(c, 0)
    g = jnp.dot(h, wg_ref[...], preferred_element_type=F32)
    u = jnp.dot(h, wu_ref[...], preferred_element_type=F32)
    act_ref[...] = ((g * _sigmoid(g)) * u).astype(act_ref.dtype)

    @pl.when(step + 1 == n_e * n_b)
    def _():
        lax.fori_loop(0, cap, drain_next, 0, unroll=DMA_LOOP_UNROLL)


def _ffn1(idx, x2, nw, wg, wu, layer, seq):
    bn, ne, cap = idx.shape
    d = x2.shape[1]
    ff = wg.shape[-1]
    idx4 = idx.reshape(bn, ne, 1, cap)
    return pl.pallas_call(
        functools.partial(_ffn1_kernel, seq=seq),
        grid=(ne, bn),
        in_specs=[
            _idx_spec(ne, bn, cap, 0), _idx_spec(ne, bn, cap, 1),
            pl.BlockSpec(memory_space=pl.ANY),
            pl.BlockSpec((1, d), lambda e, b: (0, 0)),
            pl.BlockSpec((None, None, d, ff), lambda e, b: (layer, e, 0, 0)),
            pl.BlockSpec((None, None, d, ff), lambda e, b: (layer, e, 0, 0)),
        ],
        out_specs=pl.BlockSpec((None, None, cap, ff), lambda e, b: (e, b, 0, 0)),
        out_shape=jax.ShapeDtypeStruct((ne, bn, cap, ff), BF16),
        scratch_shapes=[pltpu.VMEM((cap, d), F32), pltpu.SemaphoreType.DMA(())],
        compiler_params=_cparams("arbitrary", "arbitrary"),
        name="moe_ffn1",
    )(idx4, idx4, x2, nw.reshape(1, d), wg, wu)


def _ffn2_kernel(idx_ref, idx_next_ref, idx_prev_ref, idx_prev2_ref, gate_ref, act_ref, wd_ref, x_in, x_out,
                 xo, ye_sc, sem_in, sem_out, *, seq):
    del x_in
    cap = xo.shape[1]
    e, b = pl.program_id(0), pl.program_id(1)
    n_e, n_b = pl.num_programs(0), pl.num_programs(1)
    step = e * n_b + b
    last = n_e * n_b - 1
    slot = step % 3
    slot_next = (step + 1) % 3
    slot_prev = (step + 2) % 3

    def loop(fn):
        lax.fori_loop(0, cap, fn, 0, unroll=DMA_LOOP_UNROLL)

    def gather(ir, base, sl):
        def copy(c):
            return _row_copy(x_out, xo.at[sl], base + ir[0, c], c, sem_in.at[sl])

        def issue(c, carry):
            copy(c).start()
            return carry

        def drain(c, carry):
            copy(c).wait()
            return carry

        return issue, drain

    def scatter(ir, base, sl):
        def copy(c):
            return _row_copy(xo.at[sl], x_out, c, base + ir[0, c], sem_out.at[sl])

        def issue(c, carry):
            copy(c).start()
            return carry

        def drain(c, carry):
            copy(c).wait()
            return carry

        return issue, drain

    _, b_next = _near_step(e, b, n_e, n_b, 1)
    _, b_prev = _near_step(e, b, n_e, n_b, -1)
    _, b_prev2 = _near_step(e, b, n_e, n_b, -2)
    issue_cur, drain_cur = gather(idx_ref, b * seq, slot)
    issue_next, drain_next = gather(idx_next_ref, b_next * seq, slot_next)
    issue_put, drain_put = scatter(idx_ref, b * seq, slot)
    issue_put_prev, drain_put_prev = scatter(idx_prev_ref, b_prev * seq, slot_prev)
    _, drain_put_prev2 = scatter(idx_prev2_ref, b_prev2 * seq, slot_next)

    @pl.when(step == 0)
    def _():
        loop(issue_cur)

    @pl.when(step >= 2)
    def _():
        loop(drain_put_prev2)

    def expert_out():
        ye_sc[...] = jnp.dot(act_ref[...], wd_ref[...], preferred_element_type=F32) * gate_ref[...]

    @pl.when(step == 0)
    def _():
        for c in range(cap):
            issue_next(c, 0)
        expert_out()

    @pl.when(step > 0)
    def _():
        for c in range(cap):
            issue_put_prev(c, 0)
        for c in range(cap):
            issue_next(c, 0)
        expert_out()

    loop(drain_cur)
    xo[slot] = xo[slot] + ye_sc[...]

    @pl.when(step == last)
    def _():
        loop(issue_put)
        loop(drain_put_prev)
        loop(drain_put)
        loop(drain_next)


def _ffn2(idx, gate, act, wd, layer, x2, seq):
    bn, ne, cap = idx.shape
    assert bn >= 3
    t, d = x2.shape
    ff = wd.shape[2]
    idx4 = idx.reshape(bn, ne, 1, cap)
    return pl.pallas_call(
        functools.partial(_ffn2_kernel, seq=seq),
        grid=(ne, bn),
        in_specs=[
            _idx_spec(ne, bn, cap, 0), _idx_spec(ne, bn, cap, 1), _idx_spec(ne, bn, cap, -1),
            _idx_spec(ne, bn, cap, -2),
            pl.BlockSpec((None, None, cap, 1), lambda e, b: (b, e, 0, 0)),
            pl.BlockSpec((None, None, cap, ff), lambda e, b: (e, b, 0, 0)),
            pl.BlockSpec((None, None, ff, d), lambda e, b: (layer, e, 0, 0)),
            pl.BlockSpec(memory_space=pl.ANY),
        ],
        out_specs=pl.BlockSpec(memory_space=pl.ANY),
        out_shape=jax.ShapeDtypeStruct((t, d), F32),
        input_output_aliases={7: 0},
        scratch_shapes=[pltpu.VMEM((3, cap, d), F32), pltpu.VMEM((cap, d), F32),
                        pltpu.SemaphoreType.DMA((3,)), pltpu.SemaphoreType.DMA((3,))],
        compiler_params=_cparams("arbitrary", "arbitrary"),
        name="moe_ffn2",
    )(idx4, idx4, idx4, idx4, gate, act, wd, x2)


def kernel(x, mem, positions, w_in, w_out, norm_mix_w, norm_cross_w, norm_mem_w, norm_ffn_w, final_norm_w, ret_decay, ret_gn_w, s5_lam_re, s5_lam_im, s5_log_dt, s5_b_re, s5_b_im, s5_c_re, s5_c_im, s5_d, s5_glu_w, cross_wq, cross_wkv, cross_wo, router_w, expert_w_gate, expert_w_up, expert_w_down):
    bn, s, d = x.shape
    depth = w_in.shape[0]
    n_mem = mem.shape[1]
    width_a = 3 * d // 8
    width_b = 3 * d // 8
    width_c = d - width_a - width_b
    n_heads_a = width_a // HEAD_DIM_A
    q_off_b = 3 * width_a
    u_off = q_off_b + 2 * N_HEADS_B * LANES + 2 * width_b
    capacity = EC_CAPACITY_FACTOR * s // N_EXPERTS
    t = bn * s

    ca, sa, cbt, sbt = _rope_tables(positions)
    x2 = x.reshape(t, d)
    mem2 = mem.reshape(bn * n_mem, d)
    w_gate_b = expert_w_gate.astype(BF16)
    w_up_b = expert_w_up.astype(BF16)
    w_down_b = expert_w_down.astype(BF16)
    for l in range(depth):
        h = _rmsnorm(x2, norm_mix_w[l], BF16)
        proj = _matmul(h, w_in, l, F32)
        proj3 = proj.reshape(bn, s, -1)
        a_out = _attn_a(proj3, ca, sa, n_heads_a)
        b_out = _retention(proj3, cbt, sbt, ret_decay[l], ret_gn_w[l], q_off_b, N_HEADS_B)
        s5_tabs = _s5_tables(s5_lam_re[l], s5_lam_im[l], s5_log_dt[l], s5_b_re[l], s5_b_im[l],
                             s5_c_re[l], s5_c_im[l])
        ys = _s5(proj3[:, :, u_off:], s5_tabs)
        c_out = _s5_glu(ys, proj, u_off, s5_d[l], s5_glu_w[l].astype(BF16))
        x2 = _out_proj(a_out.reshape(t, width_a), b_out.reshape(t, width_b), c_out, w_out, l, x2)

        mn = _rmsnorm(mem2, norm_mem_w[l], BF16)
        kv = _matmul(mn, cross_wkv, l, BF16)
        a_fold, vw_fold = _cross_fold(kv.reshape(bn, n_mem, 2 * d), cross_wq, cross_wo, l, N_HEADS_X)
        x3 = x2.reshape(bn, s, d)
        probs = _cross_probs(x3, norm_cross_w[l], a_fold, N_HEADS_X)
        x2 = _bmm_res(probs, vw_fold, x3).reshape(t, d)

        logits = _router_logits(x2, norm_ffn_w[l], router_w[l].T.astype(BF16))
        idx, gate = _topk(logits, bn, capacity)
        act = _ffn1(idx, x2, norm_ffn_w[l], w_gate_b, w_up_b, l, s)
        x2 = _ffn2(idx, gate, act, w_down_b, l, x2, s)
    return _rmsnorm(x2, final_norm_w, x.dtype).reshape(bn, s, d)
```

```python
import functools
import math

import numpy as np
import jax
import jax.numpy as jnp
from jax import lax
from jax.experimental import pallas as pl
from jax.experimental.pallas import tpu as pltpu

F32 = jnp.float32
BF16 = jnp.bfloat16

EPS = 1e-6
NEG_INF = -1e30
VMEM_LIMIT_BYTES = 56 * 1024 * 1024
LANES = 128

HEAD_DIM_A = 128
ROT_DIM_A = HEAD_DIM_A // 4
ROPE_THETA = 500000.0
DILATION_PAIRS = ((128, 1), (512, 4), (2048, 16))
ATT_BLK = 128
N_HEADS_B = 6
RET_CHUNK = 128
RET_THETA = 10000.0
S5_GROUP = 16
S5_STATE = 64
S5_CHUNK = 32
N_HEADS_X = 4
N_EXPERTS = 16
EC_CAPACITY_FACTOR = 2
TOK_SPLIT = 64


def _cparams(*sem):
    return pltpu.CompilerParams(dimension_semantics=sem, vmem_limit_bytes=VMEM_LIMIT_BYTES)


def _sigmoid(x):
    return 1.0 / (1.0 + jnp.exp(-x))


def _rmsnorm_rows(x, w):
    ms = jnp.mean(x * x, axis=-1, keepdims=True)
    return x * lax.rsqrt(ms + EPS) * w


def _rmsnorm_kernel(x_ref, w_ref, o_ref):
    o_ref[...] = _rmsnorm_rows(x_ref[...], w_ref[...]).astype(o_ref.dtype)


def _rmsnorm(x2, w, out_dtype, tm=512):
    t, d = x2.shape
    tm = min(tm, t)
    return pl.pallas_call(
        _rmsnorm_kernel,
        grid=(t // tm,),
        in_specs=[pl.BlockSpec((tm, d), lambda i: (i, 0)), pl.BlockSpec((1, d), lambda i: (0, 0))],
        out_specs=pl.BlockSpec((tm, d), lambda i: (i, 0)),
        out_shape=jax.ShapeDtypeStruct((t, d), out_dtype),
        compiler_params=_cparams("parallel"),
        name="rmsnorm",
    )(x2, w.reshape(1, d))


def _router_kernel(x_ref, w_ref, rw_ref, lg_ref):
    h = _rmsnorm_rows(x_ref[...], w_ref[...]).astype(BF16)
    lg_ref[...] = lax.dot_general(rw_ref[...], h, (((1,), (1,)), ((), ())), preferred_element_type=F32)


def _router_logits(x2, w, rw_t, tm=512):
    t, d = x2.shape
    ne = rw_t.shape[0]
    return pl.pallas_call(
        _router_kernel,
        grid=(t // tm,),
        in_specs=[
            pl.BlockSpec((tm, d), lambda i: (i, 0)),
            pl.BlockSpec((1, d), lambda i: (0, 0)),
            pl.BlockSpec((ne, d), lambda i: (0, 0)),
        ],
        out_specs=pl.BlockSpec((ne, tm), lambda i: (0, i)),
        out_shape=jax.ShapeDtypeStruct((ne, t), F32),
        compiler_params=_cparams("parallel"),
        name="router_logits",
    )(x2, w.reshape(1, d), rw_t)


def _mm_ws_kernel(a_ref, w_ref, o_ref, wb):
    @pl.when(pl.program_id(1) == 0)
    def _():
        wb[...] = w_ref[...].astype(BF16)

    o_ref[...] = jnp.dot(a_ref[...], wb[...], preferred_element_type=F32).astype(o_ref.dtype)


def _mm3_ws_kernel(a1_ref, a2_ref, a3_ref, w_ref, r_ref, o_ref, wb):
    @pl.when(pl.program_id(1) == 0)
    def _():
        wb[...] = w_ref[...].astype(BF16)

    k1 = a1_ref.shape[1]
    k2 = a2_ref.shape[1]
    acc = jnp.dot(a1_ref[...], wb[0:k1, :], preferred_element_type=F32)
    acc += jnp.dot(a2_ref[...], wb[k1:k1 + k2, :], preferred_element_type=F32)
    acc += jnp.dot(a3_ref[...], wb[k1 + k2:, :], preferred_element_type=F32)
    o_ref[...] = r_ref[...] + acc


def _matmul(a, w3, layer, out_dtype, tm=1024, tn=512):
    m, k = a.shape
    n = w3.shape[2]
    tm = min(tm, m)
    tn = min(tn, n)
    return pl.pallas_call(
        _mm_ws_kernel,
        grid=(n // tn, m // tm),
        in_specs=[pl.BlockSpec((tm, k), lambda j, i: (i, 0)),
                  pl.BlockSpec((None, k, tn), lambda j, i: (layer, 0, j))],
        out_specs=pl.BlockSpec((tm, tn), lambda j, i: (i, j)),
        out_shape=jax.ShapeDtypeStruct((m, n), out_dtype),
        scratch_shapes=[pltpu.VMEM((k, tn), BF16)],
        compiler_params=_cparams("parallel", "arbitrary"),
        name="matmul",
    )(a, w3)


def _out_proj(a1, a2, a3, w3, layer, res, tm=1024, tn=512):
    m = a1.shape[0]
    k1, k2, k3 = a1.shape[1], a2.shape[1], a3.shape[1]
    k, n = w3.shape[1], w3.shape[2]
    assert k1 + k2 + k3 == k
    return pl.pallas_call(
        _mm3_ws_kernel,
        grid=(n // tn, m // tm),
        in_specs=[
            pl.BlockSpec((tm, k1), lambda j, i: (i, 0)),
            pl.BlockSpec((tm, k2), lambda j, i: (i, 0)),
            pl.BlockSpec((tm, k3), lambda j, i: (i, 0)),
            pl.BlockSpec((None, k, tn), lambda j, i: (layer, 0, j)),
            pl.BlockSpec((tm, tn), lambda j, i: (i, j)),
        ],
        out_specs=pl.BlockSpec((tm, tn), lambda j, i: (i, j)),
        out_shape=jax.ShapeDtypeStruct((m, n), F32),
        scratch_shapes=[pltpu.VMEM((k, tn), BF16)],
        compiler_params=_cparams("parallel", "arbitrary"),
        name="out_proj",
    )(a1, a2, a3, w3, res)


def _rope_kernel(pos_ref, fa_ref, fb_ref, ca_ref, sa_ref, cb_ref, sb_ref):
    pos = pos_ref[...].astype(F32)
    ang_a = pos * fa_ref[...]
    ca_ref[...] = jnp.cos(ang_a)
    sa_ref[...] = jnp.sin(ang_a)
    ang_b = pos * fb_ref[...]
    lane = lax.broadcasted_iota(jnp.int32, ang_b.shape, 1)
    sin_b = jnp.sin(ang_b)
    cb_ref[...] = jnp.cos(ang_b)
    sb_ref[...] = jnp.where(lane < LANES // 2, -sin_b, sin_b)


def _rope_tables(positions):
    b, s = positions.shape
    half_a = ROT_DIM_A // 2
    fa = ROPE_THETA ** (-(jnp.arange(half_a, dtype=F32) * 2.0 / ROT_DIM_A))
    fa_row = jnp.concatenate([fa, fa, jnp.zeros((LANES - 2 * half_a,), F32)]).reshape(1, LANES)
    fb = RET_THETA ** (-jnp.linspace(0.0, 1.0, LANES // 2, dtype=F32))
    fb_row = jnp.concatenate([fb, fb]).reshape(1, LANES)
    tab = jax.ShapeDtypeStruct((b, s, LANES), F32)
    row = pl.BlockSpec((1, LANES), lambda i: (0, 0))
    out = pl.BlockSpec((None, s, LANES), lambda i: (i, 0, 0))
    return pl.pallas_call(
        _rope_kernel,
        grid=(b,),
        in_specs=[pl.BlockSpec((None, s, 1), lambda i: (i, 0, 0)), row, row],
        out_specs=[out, out, out, out],
        out_shape=[tab, tab, tab, tab],
        compiler_params=_cparams("parallel"),
        name="rope_tables",
    )(positions.reshape(b, s, 1), fa_row, fb_row)


def _attn_a_kernel(q_ref, k_ref, v_ref, c_ref, s_ref, o_ref, qn, kn, accn, mn, ln, *, seq, radius):
    rc = 128
    scale = HEAD_DIM_A ** -0.5
    half = ROT_DIM_A // 2

    def rot_body(i, carry):
        rows = pl.ds(pl.multiple_of(i * rc, rc), rc)
        lane = lax.broadcasted_iota(jnp.int32, (rc, LANES), 1)
        c = c_ref[rows, :]
        s = s_ref[rows, :]
        s_lo = jnp.where(lane < half, -s, 0.0)
        s_hi = jnp.where(lane >= half, s, 0.0)
        q = q_ref[rows, :]
        k = k_ref[rows, :]
        qn[rows, :] = (q * c + pltpu.roll(q, LANES - half, 1) * s_lo + pltpu.roll(q, half, 1) * s_hi) * scale
        kn[rows, :] = k * c + pltpu.roll(k, LANES - half, 1) * s_lo + pltpu.roll(k, half, 1) * s_hi
        return carry

    lax.fori_loop(0, seq // rc, rot_body, 0, unroll=2)

    for branch, (_, dil) in enumerate(DILATION_PAIRS):
        sub = seq // dil
        blk = sub if sub <= 2 * ATT_BLK else ATT_BLK
        win = min(sub, blk + 2 * radius)
        nbr = sub // blk
        dmat = (lax.broadcasted_iota(jnp.int32, (blk, win), 1)
                - lax.broadcasted_iota(jnp.int32, (blk, win), 0))

        def rows_of(start, size, dil=dil):
            return pl.ds(start, size) if dil == 1 else pl.ds(start, size, stride=dil)

        def blk_body(n, carry, branch=branch, dil=dil, sub=sub, nbr=nbr, rows_of=rows_of,
                     blk=blk, win=win, dmat=dmat):
            r = n // nbr
            j = n - r * nbr
            wrel = jnp.clip(j * blk - radius, 0, sub - win)
            q_rows = rows_of(r + dil * (j * blk), blk)
            k_rows = rows_of(r + dil * wrel, win)
            qb = qn[q_rows, :].astype(BF16)
            kw = kn[k_rows, :].astype(BF16)
            vw = v_ref[k_rows, :].astype(BF16)
            sc = lax.dot_general(qb, kw, (((1,), (1,)), ((), ())), preferred_element_type=F32)
            sc = jnp.where(jnp.abs(dmat + (wrel - j * blk)) <= radius, sc, NEG_INF)
            m = jnp.max(sc, axis=1, keepdims=True)
            p = jnp.exp(sc - m)
            l = jnp.broadcast_to(jnp.sum(p, axis=1, keepdims=True), (blk, LANES))
            acc = jnp.dot(p.astype(BF16), vw, preferred_element_type=F32)
            m = jnp.broadcast_to(m, (blk, LANES))
            if branch > 0:
                m_old = mn[q_rows, :]
                m_max = jnp.maximum(m_old, m)
                e_old = jnp.exp(m_old - m_max)
                e_new = jnp.exp(m - m_max)
                acc = accn[q_rows, :] * e_old + acc * e_new
                l = ln[q_rows, :] * e_old + l * e_new
                m = m_max
            accn[q_rows, :] = acc
            mn[q_rows, :] = m
            ln[q_rows, :] = l
            return carry

        lax.fori_loop(0, seq // blk, blk_body, 0, unroll=8 if blk == ATT_BLK else 4)

    def out_body(i, carry):
        rows = pl.ds(pl.multiple_of(i * rc, rc), rc)
        o_ref[rows, :] = (accn[rows, :] / ln[rows, :]).astype(o_ref.dtype)
        return carry

    lax.fori_loop(0, seq // rc, out_body, 0, unroll=4)


def _attn_a(proj3, ca, sa, n_heads):
    b, s, _ = proj3.shape
    radii = {w // (2 * d) for w, d in DILATION_PAIRS}
    assert len(radii) == 1
    radius = radii.pop()
    for _, d in DILATION_PAIRS:
        assert s % (d * ATT_BLK) == 0 and s // d >= 2 * ATT_BLK

    def col(off):
        return pl.BlockSpec((None, s, HEAD_DIM_A), lambda i, h: (i, 0, off + h))

    tab = pl.BlockSpec((None, s, LANES), lambda i, h: (i, 0, 0))
    f32s = lambda: pltpu.VMEM((s, LANES), F32)
    return pl.pallas_call(
        functools.partial(_attn_a_kernel, seq=s, radius=radius),
        grid=(b, n_heads),
        in_specs=[col(0), col(n_heads), col(2 * n_heads), tab, tab],
        out_specs=pl.BlockSpec((None, s, HEAD_DIM_A), lambda i, h: (i, 0, h)),
        out_shape=jax.ShapeDtypeStruct((b, s, n_heads * HEAD_DIM_A), BF16),
        scratch_shapes=[f32s(), f32s(), f32s(), f32s(), f32s()],
        compiler_params=_cparams("parallel", "parallel"),
        name="attn_a",
    )(proj3, proj3, proj3, ca, sa)


def _ret_kernel(q_ref, k_ref, v_ref, g_ref, c_ref, s_ref, dm_ref, dec_ref, cdec_ref, gn_ref, o_ref,
                qn, kn, racc, sf, sb, *, seq):
    cb = RET_CHUNK
    nc = seq // cb
    dk = q_ref.shape[-1]
    dv = v_ref.shape[-1]
    h = pl.program_id(1)
    kscale = dk ** -0.5

    def rot_body(i, carry):
        rows = pl.ds(pl.multiple_of(i * cb, cb), cb)
        c = c_ref[rows, :]
        s = s_ref[rows, :]
        q = q_ref[rows, :]
        k = k_ref[rows, :]
        qn[rows, :] = q * c + pltpu.roll(q, dk // 2, 1) * s
        kn[rows, :] = (k * c + pltpu.roll(k, dk // 2, 1) * s) * kscale
        racc[rows, :] = jnp.zeros((cb, dv), F32)
        return carry

    lax.fori_loop(0, nc, rot_body, 0, unroll=2)
    sf[...] = jnp.zeros(sf.shape, F32)
    sb[...] = jnp.zeros(sb.shape, F32)
    c_f = cdec_ref[h, 0]
    c_b = cdec_ref[h, 1]

    def body(n, carry):
        rows_f = pl.ds(pl.multiple_of(n * cb, cb), cb)
        rows_b = pl.ds(pl.multiple_of((nc - 1 - n) * cb, cb), cb)
        q = qn[rows_f, :]
        k = kn[rows_f, :]
        v = v_ref[rows_f, :].astype(BF16)
        sc = lax.dot_general(q.astype(BF16), k.astype(BF16), (((1,), (1,)), ((), ())),
                             preferred_element_type=F32) * dm_ref[...]
        intra = jnp.dot(sc.astype(BF16), v, preferred_element_type=F32)
        cross = jnp.dot((q * dec_ref[0]).astype(BF16), sf[...].astype(BF16), preferred_element_type=F32)
        racc[rows_f, :] += intra + cross
        kd = (k * dec_ref[1]).T.astype(BF16)
        sf[...] = c_f * sf[...] + jnp.dot(kd, v, preferred_element_type=F32)
        q2 = qn[rows_b, :]
        k2 = kn[rows_b, :]
        v2 = v_ref[rows_b, :].astype(BF16)
        cross_b = jnp.dot((q2 * dec_ref[2]).astype(BF16), sb[...].astype(BF16), preferred_element_type=F32)
        racc[rows_b, :] += cross_b
        kd2 = (k2 * dec_ref[3]).T.astype(BF16)
        sb[...] = c_b * sb[...] + jnp.dot(kd2, v2, preferred_element_type=F32)
        return carry

    lax.fori_loop(0, nc, body, 0, unroll=4)

    def out_body(i, carry):
        rows = pl.ds(pl.multiple_of(i * cb, cb), cb)
        r = racc[rows, :]
        mu = jnp.mean(r, axis=-1, keepdims=True)
        xc = r - mu
        var = jnp.mean(xc * xc, axis=-1, keepdims=True)
        y = xc * lax.rsqrt(var + EPS) * gn_ref[...]
        g = g_ref[rows, :]
        o_ref[rows, :] = (y * (g * _sigmoid(g))).astype(o_ref.dtype)
        return carry

    lax.fori_loop(0, nc, out_body, 0, unroll=4)


def _ret_tables(ret_decay_l):
    cb = RET_CHUNK
    log_g = jax.nn.log_sigmoid(ret_decay_l.astype(F32))
    lf = log_g[0][:, None, None]
    lb = log_g[1][:, None, None]
    i = jnp.arange(cb)[:, None]
    j = jnp.arange(cb)[None, :]
    dfwd = jnp.where(i >= j, i - j, 0).astype(F32)[None]
    dbwd = jnp.where(j > i, j - i, 0).astype(F32)[None]
    dmask = jnp.where((i >= j)[None], jnp.exp(dfwd * lf), jnp.exp(dbwd * lb))
    pos = jnp.arange(cb, dtype=F32)[None, :]
    lf2, lb2 = log_g[0][:, None], log_g[1][:, None]
    dec = jnp.stack([jnp.exp((pos + 1.0) * lf2), jnp.exp((cb - 1.0 - pos) * lf2),
                     jnp.exp((cb - pos) * lb2), jnp.exp(pos * lb2)], axis=1)
    dec = jnp.broadcast_to(dec[..., None], dec.shape + (LANES,))
    cdec = jnp.exp(cb * log_g).T
    return dmask, dec, cdec


def _retention(proj3, cbt, sbt, ret_decay_l, gn_w, q_off, n_heads):
    b, s, _ = proj3.shape
    dk = LANES
    dv = 2 * dk
    dmask, dec, cdec = _ret_tables(ret_decay_l)
    qb0 = q_off // dk
    kb0 = qb0 + n_heads
    vb0 = (q_off + 2 * n_heads * dk) // dv
    gb0 = vb0 + n_heads
    tab = pl.BlockSpec((None, s, LANES), lambda i, h: (i, 0, 0))
    return pl.pallas_call(
        functools.partial(_ret_kernel, seq=s),
        grid=(b, n_heads),
        in_specs=[
            pl.BlockSpec((None, s, dk), lambda i, h: (i, 0, qb0 + h)),
            pl.BlockSpec((None, s, dk), lambda i, h: (i, 0, kb0 + h)),
            pl.BlockSpec((None, s, dv), lambda i, h: (i, 0, vb0 + h)),
            pl.BlockSpec((None, s, dv), lambda i, h: (i, 0, gb0 + h)),
            tab, tab,
            pl.BlockSpec((None, RET_CHUNK, RET_CHUNK), lambda i, h: (h, 0, 0)),
            pl.BlockSpec((None, 4, RET_CHUNK, LANES), lambda i, h: (h, 0, 0, 0)),
            pl.BlockSpec(memory_space=pltpu.SMEM),
            pl.BlockSpec((None, 1, dv), lambda i, h: (h, 0, 0)),
        ],
        out_specs=pl.BlockSpec((None, s, dv), lambda i, h: (i, 0, h)),
        out_shape=jax.ShapeDtypeStruct((b, s, n_heads * dv), BF16),
        scratch_shapes=[pltpu.VMEM((s, dk), F32), pltpu.VMEM((s, dk), F32), pltpu.VMEM((s, dv), F32),
                        pltpu.VMEM((dk, dv), F32), pltpu.VMEM((dk, dv), F32)],
        compiler_params=_cparams("parallel", "parallel"),
        name="retention",
    )(proj3, proj3, proj3, proj3, cbt, sbt, dmask, dec, cdec, gn_w.reshape(n_heads, 1, dv))


def _split_dot_nt(a, b):
    dims = (((1,), (1,)), ((), ()))
    a_hi = a.astype(BF16)
    a_lo = (a - a_hi.astype(F32)).astype(BF16)
    b_hi = b.astype(BF16)
    b_lo = (b - b_hi.astype(F32)).astype(BF16)
    out = lax.dot_general(a_hi, b_hi, dims, preferred_element_type=F32)
    out += lax.dot_general(a_hi, b_lo, dims, preferred_element_type=F32)
    out += lax.dot_general(a_lo, b_hi, dims, preferred_element_type=F32)
    return out


def _s5_kernel(u_ref, pw_ref, bb_ref, cc_ref, a_ref, y_ref, kv, hp, m_sc, e_sc, ft_sc, ca_sc, *, n_chunks, bn):
    p = S5_STATE
    c = S5_GROUP
    L = S5_CHUNK
    bbr, bbi = bb_ref[0], bb_ref[1]
    ccr, cci = cc_ref[0], cc_ref[1]
    for j in range(L):
        rows = pl.ds(j * c, c)
        pr, pi = pw_ref[0, j:j + 1, 0:2 * p], pw_ref[0, j:j + 1, 2 * p:4 * p]
        e_sc[rows, 0:2 * p] = (pr * bbr - pi * bbi).astype(BF16)
        e_sc[rows, 2 * p:4 * p] = (pr * bbi + pi * bbr).astype(BF16)
        pr, pi = pw_ref[1, j:j + 1, 0:2 * p], pw_ref[1, j:j + 1, 2 * p:4 * p]
        ft_sc[rows, 0:2 * p] = (pr * ccr - pi * cci).astype(BF16)
        ft_sc[rows, 2 * p:4 * p] = (-(pr * cci + pi * ccr)).astype(BF16)
    for s in range(2 * L):
        rows = pl.ds(s * c, c)
        pr, pi = pw_ref[2, s:s + 1, 0:2 * p], pw_ref[2, s:s + 1, 2 * p:4 * p]
        ca_sc[rows, 0:2 * p] = pr * ccr - pi * cci
        ca_sc[rows, 2 * p:4 * p] = pr * cci + pi * ccr
    strip = _split_dot_nt(jnp.concatenate([bbr, -bbi], axis=1), ca_sc[...])
    for j in range(L):
        m_sc[j * c:(j + 1) * c, :] = strip[:, (L - 1 - j) * c:(2 * L - 1 - j) * c].astype(BF16)
    u = u_ref[...]
    kv[...] = jnp.dot(u, e_sc[...], preferred_element_type=F32)
    a_re = a_ref[0:1, :]
    a_im = a_ref[1:2, :]
    is_fwd = lax.broadcasted_iota(jnp.int32, (bn, 2 * p), 1) < p
    h_re = jnp.zeros((bn, 2 * p), F32)
    h_im = jnp.zeros((bn, 2 * p), F32)
    for n in range(n_chunks):
        rf = pl.ds(n * bn, bn)
        rb = pl.ds((n_chunks - 1 - n) * bn, bn)
        hp[rf, 0:p] = h_re[:, 0:p]
        hp[rb, p:2 * p] = h_re[:, p:2 * p]
        hp[rf, 2 * p:3 * p] = h_im[:, 0:p]
        hp[rb, 3 * p:4 * p] = h_im[:, p:2 * p]
        kv_re = jnp.where(is_fwd, kv[rf, 0:2 * p], kv[rb, 0:2 * p])
        kv_im = jnp.where(is_fwd, kv[rf, 2 * p:4 * p], kv[rb, 2 * p:4 * p])
        h_re, h_im = (a_re * h_re - a_im * h_im + kv_re, a_re * h_im + a_im * h_re + kv_im)
    y = jnp.dot(u, m_sc[...], preferred_element_type=F32)
    y_ref[...] = y + lax.dot_general(hp[...].astype(BF16), ft_sc[...], (((1,), (1,)), ((), ())),
                                     preferred_element_type=F32)


def _s5_tables(lam_re, lam_im, log_dt, b_re, b_im, c_re, c_im):
    L = S5_CHUNK
    lam_re = lam_re.astype(F32)
    lam_im = lam_im.astype(F32)
    dt = jnp.exp(log_dt.astype(F32))[..., None]
    mag = jnp.exp(lam_re * dt)
    ar = mag * jnp.cos(lam_im * dt)
    ai = mag * jnp.sin(lam_im * dt)
    den = lam_re * lam_re + lam_im * lam_im
    nr = ar - 1.0
    cr = (nr * lam_re + ai * lam_im) / den
    ci = (ai * lam_re - nr * lam_im) / den
    b_re = b_re.astype(F32)
    b_im = b_im.astype(F32)
    bbr = cr[..., None] * b_re - ci[..., None] * b_im
    bbi = cr[..., None] * b_im + ci[..., None] * b_re

    def cat(x):
        return jnp.concatenate([x[0], x[1]], axis=-1)

    bb = jnp.stack([cat(bbr.transpose(0, 1, 3, 2)), cat(bbi.transpose(0, 1, 3, 2))], axis=1)
    cc = jnp.stack([cat(c_re.astype(F32)), cat(c_im.astype(F32))], axis=1)
    tau = jnp.arange(L + 1, dtype=F32)[:, None, None, None]
    pmag = jnp.exp(tau * (lam_re * dt)[None])
    pang = tau * (lam_im * dt)[None]
    pr = pmag * jnp.cos(pang)
    pi = pmag * jnp.sin(pang)

    def rows(f_idx, b_idx, f_ok=None, b_ok=None):
        f_r, f_i, b_r, b_i = pr[f_idx, 0], pi[f_idx, 0], pr[b_idx, 1], pi[b_idx, 1]
        if f_ok is not None:
            f_r, f_i = f_r * f_ok, f_i * f_ok
            b_r, b_i = b_r * b_ok, b_i * b_ok
        return jnp.concatenate([f_r, b_r, f_i, b_i], axis=-1).transpose(1, 0, 2)

    k = jnp.arange(L)
    pad = jnp.zeros((lam_re.shape[1], L, 4 * S5_STATE), F32)
    pw_e = jnp.concatenate([rows(L - 1 - k, k), pad], axis=1)
    pw_f = jnp.concatenate([rows(k + 1, L - k), pad], axis=1)
    lag = jnp.arange(2 * L) - (L - 1)
    f_ok = ((lag >= 0) & (lag < L)).astype(F32)[:, None, None]
    b_ok = (lag <= 0).astype(F32)[:, None, None]
    pw_s = rows(jnp.clip(lag, 0, L), jnp.clip(-lag, 0, L), f_ok, b_ok)
    pw = jnp.stack([pw_e, pw_f, pw_s], axis=1)
    a_l = jnp.stack([jnp.concatenate([pr[L, 0], pr[L, 1]], axis=-1),
                     jnp.concatenate([pi[L, 0], pi[L, 1]], axis=-1)], axis=1)
    return pw, bb, cc, a_l


def _s5(uc3, tables):
    pw, bb, cc, a_l = tables
    b, s, w = uc3.shape
    c = S5_GROUP
    g = w // c
    L = S5_CHUNK
    p = S5_STATE
    nc = s // L
    rows = nc * b
    ug = uc3.reshape(b, nc, L, g, c).transpose(3, 1, 0, 2, 4).reshape(g, rows, L * c).astype(BF16)
    ys = pl.pallas_call(
        functools.partial(_s5_kernel, n_chunks=nc, bn=b),
        grid=(g,),
        in_specs=[
            pl.BlockSpec((None, rows, L * c), lambda i: (i, 0, 0)),
            pl.BlockSpec((None, 3, 2 * L, 4 * p), lambda i: (i, 0, 0, 0)),
            pl.BlockSpec((None, 2, c, 2 * p), lambda i: (i, 0, 0, 0)),
            pl.BlockSpec((None, 2, c, 2 * p), lambda i: (i, 0, 0, 0)),
            pl.BlockSpec((None, 2, 2 * p), lambda i: (i, 0, 0)),
        ],
        out_specs=pl.BlockSpec((None, rows, L * c), lambda i: (i, 0, 0)),
        out_shape=jax.ShapeDtypeStruct((g, rows, L * c), F32),
        scratch_shapes=[pltpu.VMEM((rows, 4 * p), F32), pltpu.VMEM((rows, 4 * p), F32),
                        pltpu.VMEM((L * c, L * c), BF16), pltpu.VMEM((L * c, 4 * p), BF16),
                        pltpu.VMEM((L * c, 4 * p), BF16), pltpu.VMEM((2 * L * c, 4 * p), F32)],
        compiler_params=_cparams("parallel"),
        name="s5",
    )(ug, pw, bb, cc, a_l)
    return ys.reshape(g, nc, b, L, c).transpose(2, 1, 3, 0, 4).reshape(b * s, w)


def _glu_kernel(ys_ref, u_ref, d_ref, w_ref, o_ref):
    y = ys_ref[...] + u_ref[...] * d_ref[...]
    y = 0.5 * y * (1.0 + jnp.tanh(math.sqrt(2.0 / math.pi) * (y + 0.044715 * (y * y * y))))
    z = jnp.dot(y.astype(BF16), w_ref[...], preferred_element_type=F32)
    o_ref[...] = (y * _sigmoid(z)).astype(o_ref.dtype)


def _s5_glu(ys, proj2, u_off, d_w, glu_w, tm=512):
    t, w = ys.shape
    ub = u_off // w
    return pl.pallas_call(
        _glu_kernel,
        grid=(t // tm,),
        in_specs=[
            pl.BlockSpec((tm, w), lambda i: (i, 0)),
            pl.BlockSpec((tm, w), lambda i: (i, ub)),
            pl.BlockSpec((1, w), lambda i: (0, 0)),
            pl.BlockSpec((w, w), lambda i: (0, 0)),
        ],
        out_specs=pl.BlockSpec((tm, w), lambda i: (i, 0)),
        out_shape=jax.ShapeDtypeStruct((t, w), BF16),
        compiler_params=_cparams("parallel"),
        name="s5_glu",
    )(ys, proj2, d_w.reshape(1, w), glu_w)


def _cross_qk_kernel(wq_ref, km_ref, a_ref, wb):
    @pl.when(pl.program_id(1) == 0)
    def _():
        wb[...] = wq_ref[...].astype(BF16)

    a_ref[...] = lax.dot_general(wb[...], km_ref[...], (((1,), (1,)), ((), ())),
                                 preferred_element_type=F32).astype(a_ref.dtype)


def _cross_vo_kernel(vm_ref, wo_ref, vw_ref, wb):
    @pl.when(pl.program_id(1) == 0)
    def _():
        wb[...] = wo_ref[...].astype(BF16)

    vw_ref[...] = jnp.dot(vm_ref[...], wb[...], preferred_element_type=F32).astype(vw_ref.dtype)


def _cross_fold(kv3, wq3, wo3, layer, n_heads):
    b, nm, d2 = kv3.shape
    d = d2 // 2
    hd = d // n_heads
    a = pl.pallas_call(
        _cross_qk_kernel,
        grid=(n_heads, b),
        in_specs=[pl.BlockSpec((None, d, hd), lambda h, i: (layer, 0, h)),
                  pl.BlockSpec((None, nm, hd), lambda h, i: (i, 0, h))],
        out_specs=pl.BlockSpec((None, d, nm), lambda h, i: (i, 0, h)),
        out_shape=jax.ShapeDtypeStruct((b, d, n_heads * nm), BF16),
        scratch_shapes=[pltpu.VMEM((d, hd), BF16)],
        compiler_params=_cparams("parallel", "arbitrary"),
        name="cross_fold_qk",
    )(wq3, kv3)
    vw = pl.pallas_call(
        _cross_vo_kernel,
        grid=(n_heads, b),
        in_specs=[pl.BlockSpec((None, nm, hd), lambda h, i: (i, 0, n_heads + h)),
                  pl.BlockSpec((None, hd, d), lambda h, i: (layer, h, 0))],
        out_specs=pl.BlockSpec((None, nm, d), lambda h, i: (i, h, 0)),
        out_shape=jax.ShapeDtypeStruct((b, n_heads * nm, d), BF16),
        scratch_shapes=[pltpu.VMEM((hd, d), BF16)],
        compiler_params=_cparams("parallel", "arbitrary"),
        name="cross_fold_vo",
    )(kv3, wo3)
    return a, vw


def _cross_probs_kernel(x_ref, nw_ref, a_ref, p_ref, *, n_heads, scale):
    h = _rmsnorm_rows(x_ref[...], nw_ref[...]).astype(BF16)
    sc = jnp.dot(h, a_ref[...], preferred_element_type=F32) * scale
    nm = sc.shape[1] // n_heads
    for hh in range(n_heads):
        s_h = sc[:, hh * nm:(hh + 1) * nm]
        m = jnp.max(s_h, axis=-1, keepdims=True)
        e = jnp.exp(s_h - m)
        p_ref[:, hh * nm:(hh + 1) * nm] = (e / jnp.sum(e, axis=-1, keepdims=True)).astype(p_ref.dtype)


def _cross_probs(x3, nw, a, n_heads, tq=512):
    b, s, d = x3.shape
    hm = a.shape[2]
    scale = (d // n_heads) ** -0.5
    return pl.pallas_call(
        functools.partial(_cross_probs_kernel, n_heads=n_heads, scale=scale),
        grid=(b, s // tq),
        in_specs=[pl.BlockSpec((None, tq, d), lambda i, j: (i, j, 0)),
                  pl.BlockSpec((1, d), lambda i, j: (0, 0)),
                  pl.BlockSpec((None, d, hm), lambda i, j: (i, 0, 0))],
        out_specs=pl.BlockSpec((None, tq, hm), lambda i, j: (i, j, 0)),
        out_shape=jax.ShapeDtypeStruct((b, s, hm), BF16),
        compiler_params=_cparams("parallel", "parallel"),
        name="cross_probs",
    )(x3, nw.reshape(1, d), a)


def _bmm_res_kernel(a_ref, w_ref, r_ref, o_ref):
    o_ref[...] = r_ref[...] + jnp.dot(a_ref[...], w_ref[...], preferred_element_type=F32)


def _bmm_res(a3, w3, res3, tm=1024, tn=1024):
    b, s, k = a3.shape
    n = w3.shape[2]
    return pl.pallas_call(
        _bmm_res_kernel,
        grid=(b, s // tm, n // tn),
        in_specs=[pl.BlockSpec((None, tm, k), lambda i, j, c: (i, j, 0)),
                  pl.BlockSpec((None, k, tn), lambda i, j, c: (i, 0, c)),
                  pl.BlockSpec((None, tm, tn), lambda i, j, c: (i, j, c))],
        out_specs=pl.BlockSpec((None, tm, tn), lambda i, j, c: (i, j, c)),
        out_shape=jax.ShapeDtypeStruct((b, s, n), F32),
        compiler_params=_cparams("parallel", "parallel", "parallel"),
        name="cross_out",
    )(a3, w3, res3)


def _prefix_sum_lanes(x):
    n = x.shape[-1]
    lane = lax.broadcasted_iota(jnp.int32, x.shape, x.ndim - 1)
    k = 1
    while k < n:
        x = x + jnp.where(lane >= k, pltpu.roll(x, k, x.ndim - 1), 0.0)
        k *= 2
    return x


def _topk_kernel(lg_ref, idx_ref, gate_ref, *, capacity):
    lg = lg_ref[...]
    ne, s = lg.shape
    mx = jnp.max(lg, axis=0, keepdims=True)
    ex = jnp.exp(lg - mx)
    aff = ex / jnp.sum(ex, axis=0, keepdims=True)
    bits = pltpu.bitcast(aff, jnp.int32)

    def bit_body(i, thr):
        cand = thr | jnp.left_shift(jnp.int32(1), 30 - i)
        cnt = jnp.sum((bits >= cand).astype(F32), axis=1, keepdims=True)
        return jnp.where(cnt >= capacity, cand, thr)

    thr = lax.fori_loop(0, 31, bit_body, jnp.zeros((ne, 1), jnp.int32))
    gt = bits > thr
    eq = bits == thr
    n_gt = jnp.sum(gt.astype(F32), axis=1, keepdims=True)
    eq_rank = _prefix_sum_lanes(eq.astype(F32))
    sel = gt | (eq & (eq_rank <= capacity - n_gt))
    slot = _prefix_sum_lanes(sel.astype(F32)) - 1.0
    key = jnp.where(sel, slot, -1.0)
    tok = lax.broadcasted_iota(jnp.int32, (1, s), 1)
    tok_hi = (tok // TOK_SPLIT).astype(BF16)
    tok_lo = (tok % TOK_SPLIT).astype(BF16)
    slots = lax.broadcasted_iota(jnp.int32, (capacity, s), 0).astype(F32)
    for e in range(ne):
        a = aff[e:e + 1, :]
        a_hi = a.astype(BF16)
        r1 = a - a_hi.astype(F32)
        a_mid = r1.astype(BF16)
        a_lo = (r1 - a_mid.astype(F32)).astype(BF16)
        picked = jnp.concatenate([tok_hi, tok_lo, a_hi, a_mid, a_lo, jnp.zeros((3, s), BF16)], axis=0)
        onehot = jnp.where(key[e:e + 1, :] == slots, 1.0, 0.0).astype(BF16)
        res = lax.dot_general(onehot, picked, (((1,), (1,)), ((), ())), preferred_element_type=F32)
        idx_ref[e] = (res[:, 0:1] * TOK_SPLIT + res[:, 1:2]).astype(jnp.int32)
        gate_ref[e] = res[:, 2:3] + res[:, 3:4] + res[:, 4:5]


def _topk(logits, bn, capacity):
    ne, t = logits.shape
    s = t // bn
    assert s <= TOK_SPLIT * 256
    idx, gate = pl.pallas_call(
        functools.partial(_topk_kernel, capacity=capacity),
        grid=(bn,),
        in_specs=[pl.BlockSpec((ne, s), lambda i: (0, i))],
        out_specs=[pl.BlockSpec((None, ne, capacity, 1), lambda i: (i, 0, 0, 0)),
                   pl.BlockSpec((None, ne, capacity, 1), lambda i: (i, 0, 0, 0))],
        out_shape=[jax.ShapeDtypeStruct((bn, ne, capacity, 1), jnp.int32),
                   jax.ShapeDtypeStruct((bn, ne, capacity, 1), F32)],
        compiler_params=_cparams("parallel"),
        name="topk",
    )(logits)
    return idx.reshape(bn, ne, capacity), gate


def _row_copy(src, dst, src_row, dst_row, sem):
    return pltpu.make_async_copy(src.at[pl.ds(src_row, 1), :], dst.at[pl.ds(dst_row, 1), :], sem)


DMA_LOOP_UNROLL = 8


def _near_step(e, b, n_e, n_b, delta):
    lin = (e * n_b + b + delta + n_e * n_b) % (n_e * n_b)
    return lin // n_b, lin % n_b


def _idx_spec(n_e, n_b, cap, delta):
    def index_map(e, b):
        e2, b2 = _near_step(e, b, n_e, n_b, delta)
        return (b2, e2, 0, 0)

    return pl.BlockSpec((None, None, 1, cap), index_map, memory_space=pltpu.SMEM)


def _ffn1_kernel(idx_ref, idx_next_ref, x_hbm, nw_ref, wg_ref, wu_ref, act_ref, xg, sem, *, seq):
    cap = xg.shape[0]
    e, b = pl.program_id(0), pl.program_id(1)
    n_e, n_b = pl.num_programs(0), pl.num_programs(1)
    step = e * n_b + b

    def gather(ir, base):
        def issue(c, carry):
            _row_copy(x_hbm, xg, base + ir[0, c], c, sem).start()
            return carry

        def drain(c, carry):
            _row_copy(x_hbm, xg, base + ir[0, c], c, sem).wait()
            return carry

        return issue, drain

    issue_cur, drain_cur = gather(idx_ref, b * seq)
    _, b_next = _near_step(e, b, n_e, n_b, 1)
    issue_next, drain_next = gather(idx_next_ref, b_next * seq)

    @pl.when(step == 0)
    def _():
        lax.fori_loop(0, cap, issue_cur, 0, unroll=DMA_LOOP_UNROLL)

    lax.fori_loop(0, cap, drain_cur, 0, unroll=DMA_LOOP_UNROLL)
    h = _rmsnorm_rows(xg[...], nw_ref[...]).astype(BF16)

    for c in range(cap):
        issue_next(c, 0)
    g = jnp.dot(h, wg_ref[...], preferred_element_type=F32)
    u = jnp.dot(h, wu_ref[...], preferred_element_type=F32)
    act_ref[...] = ((g * _sigmoid(g)) * u).astype(act_ref.dtype)

    @pl.when(step + 1 == n_e * n_b)
    def _():
        lax.fori_loop(0, cap, drain_next, 0, unroll=DMA_LOOP_UNROLL)


def _ffn1(idx, x2, nw, wg, wu, layer, seq):
    bn, ne, cap = idx.shape
    d = x2.shape[1]
    ff = wg.shape[-1]
    idx4 = idx.reshape(bn, ne, 1, cap)
    return pl.pallas_call(
        functools.partial(_ffn1_kernel, seq=seq),
        grid=(ne, bn),
        in_specs=[
            _idx_spec(ne, bn, cap, 0), _idx_spec(ne, bn, cap, 1),
            pl.BlockSpec(memory_space=pl.ANY),
            pl.BlockSpec((1, d), lambda e, b: (0, 0)),
            pl.BlockSpec((None, None, d, ff), lambda e, b: (layer, e, 0, 0)),
            pl.BlockSpec((None, None, d, ff), lambda e, b: (layer, e, 0, 0)),
        ],
        out_specs=pl.BlockSpec((None, None, cap, ff), lambda e, b: (e, b, 0, 0)),
        out_shape=jax.ShapeDtypeStruct((ne, bn, cap, ff), BF16),
        scratch_shapes=[pltpu.VMEM((cap, d), F32), pltpu.SemaphoreType.DMA(())],
        compiler_params=_cparams("arbitrary", "arbitrary"),
        name="moe_ffn1",
    )(idx4, idx4, x2, nw.reshape(1, d), wg, wu)


def _ffn2_kernel(idx_ref, idx_next_ref, idx_prev_ref, idx_prev2_ref, gate_ref, act_ref, wd_ref, x_in, x_out,
                 xo, ye_sc, sem_in, sem_out, *, seq):
    del x_in
    cap = xo.shape[1]
    e, b = pl.program_id(0), pl.program_id(1)
    n_e, n_b = pl.num_programs(0), pl.num_programs(1)
    step = e * n_b + b
    last = n_e * n_b - 1
    slot = step % 3
    slot_next = (step + 1) % 3
    slot_prev = (step + 2) % 3

    def loop(fn):
        lax.fori_loop(0, cap, fn, 0, unroll=DMA_LOOP_UNROLL)

    def gather(ir, base, sl):
        def copy(c):
            return _row_copy(x_out, xo.at[sl], base + ir[0, c], c, sem_in.at[sl])

        def issue(c, carry):
            copy(c).start()
            return carry

        def drain(c, carry):
            copy(c).wait()
            return carry

        return issue, drain

    def scatter(ir, base, sl):
        def copy(c):
            return _row_copy(xo.at[sl], x_out, c, base + ir[0, c], sem_out.at[sl])

        def issue(c, carry):
            copy(c).start()
            return carry

        def drain(c, carry):
            copy(c).wait()
            return carry

        return issue, drain

    _, b_next = _near_step(e, b, n_e, n_b, 1)
    _, b_prev = _near_step(e, b, n_e, n_b, -1)
    _, b_prev2 = _near_step(e, b, n_e, n_b, -2)
    issue_cur, drain_cur = gather(idx_ref, b * seq, slot)
    issue_next, drain_next = gather(idx_next_ref, b_next * seq, slot_next)
    issue_put, drain_put = scatter(idx_ref, b * seq, slot)
    issue_put_prev, drain_put_prev = scatter(idx_prev_ref, b_prev * seq, slot_prev)
    _, drain_put_prev2 = scatter(idx_prev2_ref, b_prev2 * seq, slot_next)

    @pl.when(step == 0)
    def _():
        loop(issue_cur)

    @pl.when(step >= 2)
    def _():
        loop(drain_put_prev2)

    def expert_out():
        ye_sc[...] = jnp.dot(act_ref[...], wd_ref[...], preferred_element_type=F32) * gate_ref[...]

    @pl.when(step == 0)
    def _():
        for c in range(cap):
            issue_next(c, 0)
        expert_out()

    @pl.when(step > 0)
    def _():
        for c in range(cap):
            issue_put_prev(c, 0)
        for c in range(cap):
            issue_next(c, 0)
        expert_out()

    loop(drain_cur)
    xo[slot] = xo[slot] + ye_sc[...]

    @pl.when(step == last)
    def _():
        loop(issue_put)
        loop(drain_put_prev)
        loop(drain_put)
        loop(drain_next)


def _ffn2(idx, gate, act, wd, layer, x2, seq):
    bn, ne, cap = idx.shape
    assert bn >= 3
    t, d = x2.shape
    ff = wd.shape[2]
    idx4 = idx.reshape(bn, ne, 1, cap)
    return pl.pallas_call(
        functools.partial(_ffn2_kernel, seq=seq),
        grid=(ne, bn),
        in_specs=[
            _idx_spec(ne, bn, cap, 0), _idx_spec(ne, bn, cap, 1), _idx_spec(ne, bn, cap, -1),
            _idx_spec(ne, bn, cap, -2),
            pl.BlockSpec((None, None, cap, 1), lambda e, b: (b, e, 0, 0)),
            pl.BlockSpec((None, None, cap, ff), lambda e, b: (e, b, 0, 0)),
            pl.BlockSpec((None, None, ff, d), lambda e, b: (layer, e, 0, 0)),
            pl.BlockSpec(memory_space=pl.ANY),
        ],
        out_specs=pl.BlockSpec(memory_space=pl.ANY),
        out_shape=jax.ShapeDtypeStruct((t, d), F32),
        input_output_aliases={7: 0},
        scratch_shapes=[pltpu.VMEM((3, cap, d), F32), pltpu.VMEM((cap, d), F32),
                        pltpu.SemaphoreType.DMA((3,)), pltpu.SemaphoreType.DMA((3,))],
        compiler_params=_cparams("arbitrary", "arbitrary"),
        name="moe_ffn2",
    )(idx4, idx4, idx4, idx4, gate, act, wd, x2)


def kernel(x, mem, positions, w_in, w_out, norm_mix_w, norm_cross_w, norm_mem_w, norm_ffn_w, final_norm_w, ret_decay, ret_gn_w, s5_lam_re, s5_lam_im, s5_log_dt, s5_b_re, s5_b_im, s5_c_re, s5_c_im, s5_d, s5_glu_w, cross_wq, cross_wkv, cross_wo, router_w, expert_w_gate, expert_w_up, expert_w_down):
    bn, s, d = x.shape
    depth = w_in.shape[0]
    n_mem = mem.shape[1]
    width_a = 3 * d // 8
    width_b = 3 * d // 8
    width_c = d - width_a - width_b
    n_heads_a = width_a // HEAD_DIM_A
    q_off_b = 3 * width_a
    u_off = q_off_b + 2 * N_HEADS_B * LANES + 2 * width_b
    capacity = EC_CAPACITY_FACTOR * s // N_EXPERTS
    t = bn * s

    ca, sa, cbt, sbt = _rope_tables(positions)
    x2 = x.reshape(t, d)
    mem2 = mem.reshape(bn * n_mem, d)
    w_gate_b = expert_w_gate.astype(BF16)
    w_up_b = expert_w_up.astype(BF16)
    w_down_b = expert_w_down.astype(BF16)
    for l in range(depth):
        h = _rmsnorm(x2, norm_mix_w[l], BF16)
        proj = _matmul(h, w_in, l, F32)
        proj3 = proj.reshape(bn, s, -1)
        a_out = _attn_a(proj3, ca, sa, n_heads_a)
        b_out = _retention(proj3, cbt, sbt, ret_decay[l], ret_gn_w[l], q_off_b, N_HEADS_B)
        s5_tabs = _s5_tables(s5_lam_re[l], s5_lam_im[l], s5_log_dt[l], s5_b_re[l], s5_b_im[l],
                             s5_c_re[l], s5_c_im[l])
        ys = _s5(proj3[:, :, u_off:], s5_tabs)
        c_out = _s5_glu(ys, proj, u_off, s5_d[l], s5_glu_w[l].astype(BF16))
        x2 = _out_proj(a_out.reshape(t, width_a), b_out.reshape(t, width_b), c_out, w_out, l, x2)

        mn = _rmsnorm(mem2, norm_mem_w[l], BF16)
        kv = _matmul(mn, cross_wkv, l, BF16)
        a_fold, vw_fold = _cross_fold(kv.reshape(bn, n_mem, 2 * d), cross_wq, cross_wo, l, N_HEADS_X)
        x3 = x2.reshape(bn, s, d)
        probs = _cross_probs(x3, norm_cross_w[l], a_fold, N_HEADS_X)
        x2 = _bmm_res(probs, vw_fold, x3).reshape(t, d)

        logits = _router_logits(x2, norm_ffn_w[l], router_w[l].T.astype(BF16))
        idx, gate = _topk(logits, bn, capacity)
        act = _ffn1(idx, x2, norm_ffn_w[l], w_gate_b, w_up_b, l, s)
        x2 = _ffn2(idx, gate, act, w_down_b, l, x2, s)
    return _rmsnorm(x2, final_norm_w, x.dtype).reshape(bn, s, d)
```

```python
import functools
import math

import numpy as np
import jax
import jax.numpy as jnp
from jax import lax
from jax.experimental import pallas as pl
from jax.experimental.pallas import tpu as pltpu

F32 = jnp.float32
BF16 = jnp.bfloat16

EPS = 1e-6
NEG_INF = -1e30
VMEM_LIMIT_BYTES = 56 * 1024 * 1024
LANES = 128

HEAD_DIM_A = 128
ROT_DIM_A = HEAD_DIM_A // 4
ROPE_THETA = 500000.0
DILATION_PAIRS = ((128, 1), (512, 4), (2048, 16))
ATT_BLK = 128
N_HEADS_B = 6
RET_CHUNK = 128
RET_THETA = 10000.0
S5_GROUP = 16
S5_STATE = 64
S5_CHUNK = 32
N_HEADS_X = 4
N_EXPERTS = 16
EC_CAPACITY_FACTOR = 2
TOK_SPLIT = 64


def _cparams(*sem):
    return pltpu.CompilerParams(dimension_semantics=sem, vmem_limit_bytes=VMEM_LIMIT_BYTES)


def _sigmoid(x):
    return 1.0 / (1.0 + jnp.exp(-x))


def _rmsnorm_rows(x, w):
    ms = jnp.mean(x * x, axis=-1, keepdims=True)
    return x * lax.rsqrt(ms + EPS) * w


def _rmsnorm_kernel(x_ref, w_ref, o_ref):
    o_ref[...] = _rmsnorm_rows(x_ref[...], w_ref[...]).astype(o_ref.dtype)


def _rmsnorm(x2, w, out_dtype, tm=512):
    t, d = x2.shape
    tm = min(tm, t)
    return pl.pallas_call(
        _rmsnorm_kernel,
        grid=(t // tm,),
        in_specs=[pl.BlockSpec((tm, d), lambda i: (i, 0)), pl.BlockSpec((1, d), lambda i: (0, 0))],
        out_specs=pl.BlockSpec((tm, d), lambda i: (i, 0)),
        out_shape=jax.ShapeDtypeStruct((t, d), out_dtype),
        compiler_params=_cparams("parallel"),
        name="rmsnorm",
    )(x2, w.reshape(1, d))


def _mm_ws_kernel(a_ref, w_ref, o_ref, wb):
    @pl.when(pl.program_id(1) == 0)
    def _():
        wb[...] = w_ref[...].astype(BF16)

    o_ref[...] = jnp.dot(a_ref[...], wb[...], preferred_element_type=F32).astype(o_ref.dtype)


def _mm3_ws_kernel(a1_ref, a2_ref, a3_ref, w_ref, r_ref, o_ref, wb):
    @pl.when(pl.program_id(1) == 0)
    def _():
        wb[...] = w_ref[...].astype(BF16)

    k1 = a1_ref.shape[1]
    k2 = a2_ref.shape[1]
    acc = jnp.dot(a1_ref[...], wb[0:k1, :], preferred_element_type=F32)
    acc += jnp.dot(a2_ref[...], wb[k1:k1 + k2, :], preferred_element_type=F32)
    acc += jnp.dot(a3_ref[...], wb[k1 + k2:, :], preferred_element_type=F32)
    o_ref[...] = r_ref[...] + acc


def _matmul(a, w3, layer, out_dtype, tm=1024, tn=512):
    m, k = a.shape
    n = w3.shape[2]
    tm = min(tm, m)
    tn = min(tn, n)
    return pl.pallas_call(
        _mm_ws_kernel,
        grid=(n // tn, m // tm),
        in_specs=[pl.BlockSpec((tm, k), lambda j, i: (i, 0)),
                  pl.BlockSpec((None, k, tn), lambda j, i: (layer, 0, j))],
        out_specs=pl.BlockSpec((tm, tn), lambda j, i: (i, j)),
        out_shape=jax.ShapeDtypeStruct((m, n), out_dtype),
        scratch_shapes=[pltpu.VMEM((k, tn), BF16)],
        compiler_params=_cparams("parallel", "arbitrary"),
        name="matmul",
    )(a, w3)


def _out_proj(a1, a2, a3, w3, layer, res, tm=1024, tn=512):
    m = a1.shape[0]
    k1, k2, k3 = a1.shape[1], a2.shape[1], a3.shape[1]
    k, n = w3.shape[1], w3.shape[2]
    assert k1 + k2 + k3 == k
    return pl.pallas_call(
        _mm3_ws_kernel,
        grid=(n // tn, m // tm),
        in_specs=[
            pl.BlockSpec((tm, k1), lambda j, i: (i, 0)),
            pl.BlockSpec((tm, k2), lambda j, i: (i, 0)),
            pl.BlockSpec((tm, k3), lambda j, i: (i, 0)),
            pl.BlockSpec((None, k, tn), lambda j, i: (layer, 0, j)),
            pl.BlockSpec((tm, tn), lambda j, i: (i, j)),
        ],
        out_specs=pl.BlockSpec((tm, tn), lambda j, i: (i, j)),
        out_shape=jax.ShapeDtypeStruct((m, n), F32),
        scratch_shapes=[pltpu.VMEM((k, tn), BF16)],
        compiler_params=_cparams("parallel", "arbitrary"),
        name="out_proj",
    )(a1, a2, a3, w3, res)


def _rope_kernel(pos_ref, fa_ref, fb_ref, ca_ref, sa_ref, cb_ref, sb_ref):
    pos = pos_ref[...].astype(F32)
    ang_a = pos * fa_ref[...]
    ca_ref[...] = jnp.cos(ang_a)
    sa_ref[...] = jnp.sin(ang_a)
    ang_b = pos * fb_ref[...]
    lane = lax.broadcasted_iota(jnp.int32, ang_b.shape, 1)
    sin_b = jnp.sin(ang_b)
    cb_ref[...] = jnp.cos(ang_b)
    sb_ref[...] = jnp.where(lane < LANES // 2, -sin_b, sin_b)


def _rope_tables(positions):
    b, s = positions.shape
    half_a = ROT_DIM_A // 2
    fa = ROPE_THETA ** (-(jnp.arange(half_a, dtype=F32) * 2.0 / ROT_DIM_A))
    fa_row = jnp.concatenate([fa, fa, jnp.zeros((LANES - 2 * half_a,), F32)]).reshape(1, LANES)
    fb = RET_THETA ** (-jnp.linspace(0.0, 1.0, LANES // 2, dtype=F32))
    fb_row = jnp.concatenate([fb, fb]).reshape(1, LANES)
    tab = jax.ShapeDtypeStruct((b, s, LANES), F32)
    row = pl.BlockSpec((1, LANES), lambda i: (0, 0))
    out = pl.BlockSpec((None, s, LANES), lambda i: (i, 0, 0))
    return pl.pallas_call(
        _rope_kernel,
        grid=(b,),
        in_specs=[pl.BlockSpec((None, s, 1), lambda i: (i, 0, 0)), row, row],
        out_specs=[out, out, out, out],
        out_shape=[tab, tab, tab, tab],
        compiler_params=_cparams("parallel"),
        name="rope_tables",
    )(positions.reshape(b, s, 1), fa_row, fb_row)


def _attn_a_kernel(q_ref, k_ref, v_ref, c_ref, s_ref, o_ref, qn, kn, accn, mn, ln, *, seq, radius):
    rc = 128
    scale = HEAD_DIM_A ** -0.5
    half = ROT_DIM_A // 2

    def rot_body(i, carry):
        rows = pl.ds(pl.multiple_of(i * rc, rc), rc)
        lane = lax.broadcasted_iota(jnp.int32, (rc, LANES), 1)
        c = c_ref[rows, :]
        s = s_ref[rows, :]
        s_lo = jnp.where(lane < half, -s, 0.0)
        s_hi = jnp.where(lane >= half, s, 0.0)
        q = q_ref[rows, :]
        k = k_ref[rows, :]
        qn[rows, :] = (q * c + pltpu.roll(q, LANES - half, 1) * s_lo + pltpu.roll(q, half, 1) * s_hi) * scale
        kn[rows, :] = k * c + pltpu.roll(k, LANES - half, 1) * s_lo + pltpu.roll(k, half, 1) * s_hi
        return carry

    lax.fori_loop(0, seq // rc, rot_body, 0, unroll=2)

    for branch, (_, dil) in enumerate(DILATION_PAIRS):
        sub = seq // dil
        blk = sub if sub <= 2 * ATT_BLK else ATT_BLK
        win = min(sub, blk + 2 * radius)
        nbr = sub // blk
        dmat = (lax.broadcasted_iota(jnp.int32, (blk, win), 1)
                - lax.broadcasted_iota(jnp.int32, (blk, win), 0))

        def rows_of(start, size, dil=dil):
            return pl.ds(start, size) if dil == 1 else pl.ds(start, size, stride=dil)

        def blk_body(n, carry, branch=branch, dil=dil, sub=sub, nbr=nbr, rows_of=rows_of,
                     blk=blk, win=win, dmat=dmat):
            r = n // nbr
            j = n - r * nbr
            wrel = jnp.clip(j * blk - radius, 0, sub - win)
            q_rows = rows_of(r + dil * (j * blk), blk)
            k_rows = rows_of(r + dil * wrel, win)
            qb = qn[q_rows, :].astype(BF16)
            kw = kn[k_rows, :].astype(BF16)
            vw = v_ref[k_rows, :].astype(BF16)
            sc = lax.dot_general(qb, kw, (((1,), (1,)), ((), ())), preferred_element_type=F32)
            sc = jnp.where(jnp.abs(dmat + (wrel - j * blk)) <= radius, sc, NEG_INF)
            m = jnp.max(sc, axis=1, keepdims=True)
            p = jnp.exp(sc - m)
            l = jnp.broadcast_to(jnp.sum(p, axis=1, keepdims=True), (blk, LANES))
            acc = jnp.dot(p.astype(BF16), vw, preferred_element_type=F32)
            m = jnp.broadcast_to(m, (blk, LANES))
            if branch > 0:
                m_old = mn[q_rows, :]
                m_max = jnp.maximum(m_old, m)
                e_old = jnp.exp(m_old - m_max)
                e_new = jnp.exp(m - m_max)
                acc = accn[q_rows, :] * e_old + acc * e_new
                l = ln[q_rows, :] * e_old + l * e_new
                m = m_max
            accn[q_rows, :] = acc
            mn[q_rows, :] = m
            ln[q_rows, :] = l
            return carry

        lax.fori_loop(0, seq // blk, blk_body, 0, unroll=16 if blk == ATT_BLK else 8)

    def out_body(i, carry):
        rows = pl.ds(pl.multiple_of(i * rc, rc), rc)
        o_ref[rows, :] = (accn[rows, :] / ln[rows, :]).astype(o_ref.dtype)
        return carry

    lax.fori_loop(0, seq // rc, out_body, 0, unroll=4)


def _attn_a(proj3, ca, sa, n_heads):
    b, s, _ = proj3.shape
    radii = {w // (2 * d) for w, d in DILATION_PAIRS}
    assert len(radii) == 1
    radius = radii.pop()
    for _, d in DILATION_PAIRS:
        assert s % (d * ATT_BLK) == 0 and s // d >= 2 * ATT_BLK

    def col(off):
        return pl.BlockSpec((None, s, HEAD_DIM_A), lambda i, h: (i, 0, off + h))

    tab = pl.BlockSpec((None, s, LANES), lambda i, h: (i, 0, 0))
    f32s = lambda: pltpu.VMEM((s, LANES), F32)
    return pl.pallas_call(
        functools.partial(_attn_a_kernel, seq=s, radius=radius),
        grid=(b, n_heads),
        in_specs=[col(0), col(n_heads), col(2 * n_heads), tab, tab],
        out_specs=pl.BlockSpec((None, s, HEAD_DIM_A), lambda i, h: (i, 0, h)),
        out_shape=jax.ShapeDtypeStruct((b, s, n_heads * HEAD_DIM_A), BF16),
        scratch_shapes=[f32s(), f32s(), f32s(), f32s(), f32s()],
        compiler_params=_cparams("parallel", "parallel"),
        name="attn_a",
    )(proj3, proj3, proj3, ca, sa)


def _ret_kernel(q_ref, k_ref, v_ref, g_ref, c_ref, s_ref, dm_ref, dec_ref, cdec_ref, gn_ref, o_ref,
                qn, kn, racc, sf, sb, *, seq):
    cb = RET_CHUNK
    nc = seq // cb
    dk = q_ref.shape[-1]
    dv = v_ref.shape[-1]
    h = pl.program_id(1)
    kscale = dk ** -0.5

    def rot_body(i, carry):
        rows = pl.ds(pl.multiple_of(i * cb, cb), cb)
        c = c_ref[rows, :]
        s = s_ref[rows, :]
        q = q_ref[rows, :]
        k = k_ref[rows, :]
        qn[rows, :] = q * c + pltpu.roll(q, dk // 2, 1) * s
        kn[rows, :] = (k * c + pltpu.roll(k, dk // 2, 1) * s) * kscale
        racc[rows, :] = jnp.zeros((cb, dv), F32)
        return carry

    lax.fori_loop(0, nc, rot_body, 0, unroll=2)
    sf[...] = jnp.zeros(sf.shape, F32)
    sb[...] = jnp.zeros(sb.shape, F32)
    c_f = cdec_ref[h, 0]
    c_b = cdec_ref[h, 1]

    def body(n, carry):
        rows_f = pl.ds(pl.multiple_of(n * cb, cb), cb)
        rows_b = pl.ds(pl.multiple_of((nc - 1 - n) * cb, cb), cb)
        q = qn[rows_f, :]
        k = kn[rows_f, :]
        v = v_ref[rows_f, :].astype(BF16)
        sc = lax.dot_general(q.astype(BF16), k.astype(BF16), (((1,), (1,)), ((), ())),
                             preferred_element_type=F32) * dm_ref[...]
        intra = jnp.dot(sc.astype(BF16), v, preferred_element_type=F32)
        cross = jnp.dot((q * dec_ref[0]).astype(BF16), sf[...].astype(BF16), preferred_element_type=F32)
        racc[rows_f, :] += intra + cross
        kd = (k * dec_ref[1]).T.astype(BF16)
        sf[...] = c_f * sf[...] + jnp.dot(kd, v, preferred_element_type=F32)
        q2 = qn[rows_b, :]
        k2 = kn[rows_b, :]
        v2 = v_ref[rows_b, :].astype(BF16)
        cross_b = jnp.dot((q2 * dec_ref[2]).astype(BF16), sb[...].astype(BF16), preferred_element_type=F32)
        racc[rows_b, :] += cross_b
        kd2 = (k2 * dec_ref[3]).T.astype(BF16)
        sb[...] = c_b * sb[...] + jnp.dot(kd2, v2, preferred_element_type=F32)
        return carry

    lax.fori_loop(0, nc, body, 0, unroll=4)

    def out_body(i, carry):
        rows = pl.ds(pl.multiple_of(i * cb, cb), cb)
        r = racc[rows, :]
        mu = jnp.mean(r, axis=-1, keepdims=True)
        xc = r - mu
        var = jnp.mean(xc * xc, axis=-1, keepdims=True)
        y = xc * lax.rsqrt(var + EPS) * gn_ref[...]
        g = g_ref[rows, :]
        o_ref[rows, :] = (y * (g * _sigmoid(g))).astype(o_ref.dtype)
        return carry

    lax.fori_loop(0, nc, out_body, 0, unroll=4)


def _ret_tables(ret_decay_l):
    cb = RET_CHUNK
    log_g = jax.nn.log_sigmoid(ret_decay_l.astype(F32))
    lf = log_g[0][:, None, None]
    lb = log_g[1][:, None, None]
    i = jnp.arange(cb)[:, None]
    j = jnp.arange(cb)[None, :]
    dfwd = jnp.where(i >= j, i - j, 0).astype(F32)[None]
    dbwd = jnp.where(j > i, j - i, 0).astype(F32)[None]
    dmask = jnp.where((i >= j)[None], jnp.exp(dfwd * lf), jnp.exp(dbwd * lb))
    pos = jnp.arange(cb, dtype=F32)[None, :]
    lf2, lb2 = log_g[0][:, None], log_g[1][:, None]
    dec = jnp.stack([jnp.exp((pos + 1.0) * lf2), jnp.exp((cb - 1.0 - pos) * lf2),
                     jnp.exp((cb - pos) * lb2), jnp.exp(pos * lb2)], axis=1)
    dec = jnp.broadcast_to(dec[..., None], dec.shape + (LANES,))
    cdec = jnp.exp(cb * log_g).T
    return dmask, dec, cdec


def _retention(proj3, cbt, sbt, ret_decay_l, gn_w, q_off, n_heads):
    b, s, _ = proj3.shape
    dk = LANES
    dv = 2 * dk
    dmask, dec, cdec = _ret_tables(ret_decay_l)
    qb0 = q_off // dk
    kb0 = qb0 + n_heads
    vb0 = (q_off + 2 * n_heads * dk) // dv
    gb0 = vb0 + n_heads
    tab = pl.BlockSpec((None, s, LANES), lambda i, h: (i, 0, 0))
    return pl.pallas_call(
        functools.partial(_ret_kernel, seq=s),
        grid=(b, n_heads),
        in_specs=[
            pl.BlockSpec((None, s, dk), lambda i, h: (i, 0, qb0 + h)),
            pl.BlockSpec((None, s, dk), lambda i, h: (i, 0, kb0 + h)),
            pl.BlockSpec((None, s, dv), lambda i, h: (i, 0, vb0 + h)),
            pl.BlockSpec((None, s, dv), lambda i, h: (i, 0, gb0 + h)),
            tab, tab,
            pl.BlockSpec((None, RET_CHUNK, RET_CHUNK), lambda i, h: (h, 0, 0)),
            pl.BlockSpec((None, 4, RET_CHUNK, LANES), lambda i, h: (h, 0, 0, 0)),
            pl.BlockSpec(memory_space=pltpu.SMEM),
            pl.BlockSpec((None, 1, dv), lambda i, h: (h, 0, 0)),
        ],
        out_specs=pl.BlockSpec((None, s, dv), lambda i, h: (i, 0, h)),
        out_shape=jax.ShapeDtypeStruct((b, s, n_heads * dv), BF16),
        scratch_shapes=[pltpu.VMEM((s, dk), F32), pltpu.VMEM((s, dk), F32), pltpu.VMEM((s, dv), F32),
                        pltpu.VMEM((dk, dv), F32), pltpu.VMEM((dk, dv), F32)],
        compiler_params=_cparams("parallel", "parallel"),
        name="retention",
    )(proj3, proj3, proj3, proj3, cbt, sbt, dmask, dec, cdec, gn_w.reshape(n_heads, 1, dv))


def _split_dot_nt(a, b):
    dims = (((1,), (1,)), ((), ()))
    a_hi = a.astype(BF16)
    a_lo = (a - a_hi.astype(F32)).astype(BF16)
    b_hi = b.astype(BF16)
    b_lo = (b - b_hi.astype(F32)).astype(BF16)
    out = lax.dot_general(a_hi, b_hi, dims, preferred_element_type=F32)
    out += lax.dot_general(a_hi, b_lo, dims, preferred_element_type=F32)
    out += lax.dot_general(a_lo, b_hi, dims, preferred_element_type=F32)
    return out


def _s5_kernel(u_ref, pw_ref, bb_ref, cc_ref, a_ref, y_ref, kv, hp, m_sc, e_sc, ft_sc, ca_sc, *, n_chunks, bn):
    p = S5_STATE
    c = S5_GROUP
    L = S5_CHUNK
    bbr, bbi = bb_ref[0], bb_ref[1]
    ccr, cci = cc_ref[0], cc_ref[1]
    for j in range(L):
        rows = pl.ds(j * c, c)
        pr, pi = pw_ref[0, j:j + 1, 0:2 * p], pw_ref[0, j:j + 1, 2 * p:4 * p]
        e_sc[rows, 0:2 * p] = (pr * bbr - pi * bbi).astype(BF16)
        e_sc[rows, 2 * p:4 * p] = (pr * bbi + pi * bbr).astype(BF16)
        pr, pi = pw_ref[1, j:j + 1, 0:2 * p], pw_ref[1, j:j + 1, 2 * p:4 * p]
        ft_sc[rows, 0:2 * p] = (pr * ccr - pi * cci).astype(BF16)
        ft_sc[rows, 2 * p:4 * p] = (-(pr * cci + pi * ccr)).astype(BF16)
    for s in range(2 * L):
        rows = pl.ds(s * c, c)
        pr, pi = pw_ref[2, s:s + 1, 0:2 * p], pw_ref[2, s:s + 1, 2 * p:4 * p]
        ca_sc[rows, 0:2 * p] = pr * ccr - pi * cci
        ca_sc[rows, 2 * p:4 * p] = pr * cci + pi * ccr
    strip = _split_dot_nt(jnp.concatenate([bbr, -bbi], axis=1), ca_sc[...])
    for j in range(L):
        m_sc[j * c:(j + 1) * c, :] = strip[:, (L - 1 - j) * c:(2 * L - 1 - j) * c].astype(BF16)
    u = u_ref[...]
    kv[...] = jnp.dot(u, e_sc[...], preferred_element_type=F32)
    a_re = a_ref[0:1, :]
    a_im = a_ref[1:2, :]
    is_fwd = lax.broadcasted_iota(jnp.int32, (bn, 2 * p), 1) < p
    h_re = jnp.zeros((bn, 2 * p), F32)
    h_im = jnp.zeros((bn, 2 * p), F32)
    for n in range(n_chunks):
        rf = pl.ds(n * bn, bn)
        rb = pl.ds((n_chunks - 1 - n) * bn, bn)
        hp[rf, 0:p] = h_re[:, 0:p]
        hp[rb, p:2 * p] = h_re[:, p:2 * p]
        hp[rf, 2 * p:3 * p] = h_im[:, 0:p]
        hp[rb, 3 * p:4 * p] = h_im[:, p:2 * p]
        kv_re = jnp.where(is_fwd, kv[rf, 0:2 * p], kv[rb, 0:2 * p])
        kv_im = jnp.where(is_fwd, kv[rf, 2 * p:4 * p], kv[rb, 2 * p:4 * p])
        h_re, h_im = (a_re * h_re - a_im * h_im + kv_re, a_re * h_im + a_im * h_re + kv_im)
    y = jnp.dot(u, m_sc[...], preferred_element_type=F32)
    y_ref[...] = y + lax.dot_general(hp[...].astype(BF16), ft_sc[...], (((1,), (1,)), ((), ())),
                                     preferred_element_type=F32)


def _s5_tables(lam_re, lam_im, log_dt, b_re, b_im, c_re, c_im):
    L = S5_CHUNK
    lam_re = lam_re.astype(F32)
    lam_im = lam_im.astype(F32)
    dt = jnp.exp(log_dt.astype(F32))[..., None]
    mag = jnp.exp(lam_re * dt)
    ar = mag * jnp.cos(lam_im * dt)
    ai = mag * jnp.sin(lam_im * dt)
    den = lam_re * lam_re + lam_im * lam_im
    nr = ar - 1.0
    cr = (nr * lam_re + ai * lam_im) / den
    ci = (ai * lam_re - nr * lam_im) / den
    b_re = b_re.astype(F32)
    b_im = b_im.astype(F32)
    bbr = cr[..., None] * b_re - ci[..., None] * b_im
    bbi = cr[..., None] * b_im + ci[..., None] * b_re

    def cat(x):
        return jnp.concatenate([x[0], x[1]], axis=-1)

    bb = jnp.stack([cat(bbr.transpose(0, 1, 3, 2)), cat(bbi.transpose(0, 1, 3, 2))], axis=1)
    cc = jnp.stack([cat(c_re.astype(F32)), cat(c_im.astype(F32))], axis=1)
    tau = jnp.arange(L + 1, dtype=F32)[:, None, None, None]
    pmag = jnp.exp(tau * (lam_re * dt)[None])
    pang = tau * (lam_im * dt)[None]
    pr = pmag * jnp.cos(pang)
    pi = pmag * jnp.sin(pang)

    def rows(f_idx, b_idx, f_ok=None, b_ok=None):
        f_r, f_i, b_r, b_i = pr[f_idx, 0], pi[f_idx, 0], pr[b_idx, 1], pi[b_idx, 1]
        if f_ok is not None:
            f_r, f_i = f_r * f_ok, f_i * f_ok
            b_r, b_i = b_r * b_ok, b_i * b_ok
        return jnp.concatenate([f_r, b_r, f_i, b_i], axis=-1).transpose(1, 0, 2)

    k = jnp.arange(L)
    pad = jnp.zeros((lam_re.shape[1], L, 4 * S5_STATE), F32)
    pw_e = jnp.concatenate([rows(L - 1 - k, k), pad], axis=1)
    pw_f = jnp.concatenate([rows(k + 1, L - k), pad], axis=1)
    lag = jnp.arange(2 * L) - (L - 1)
    f_ok = ((lag >= 0) & (lag < L)).astype(F32)[:, None, None]
    b_ok = (lag <= 0).astype(F32)[:, None, None]
    pw_s = rows(jnp.clip(lag, 0, L), jnp.clip(-lag, 0, L), f_ok, b_ok)
    pw = jnp.stack([pw_e, pw_f, pw_s], axis=1)
    a_l = jnp.stack([jnp.concatenate([pr[L, 0], pr[L, 1]], axis=-1),
                     jnp.concatenate([pi[L, 0], pi[L, 1]], axis=-1)], axis=1)
    return pw, bb, cc, a_l


def _s5(uc3, tables):
    pw, bb, cc, a_l = tables
    b, s, w = uc3.shape
    c = S5_GROUP
    g = w // c
    L = S5_CHUNK
    p = S5_STATE
    nc = s // L
    rows = nc * b
    ug = uc3.reshape(b, nc, L, g, c).transpose(3, 1, 0, 2, 4).reshape(g, rows, L * c).astype(BF16)
    ys = pl.pallas_call(
        functools.partial(_s5_kernel, n_chunks=nc, bn=b),
        grid=(g,),
        in_specs=[
            pl.BlockSpec((None, rows, L * c), lambda i: (i, 0, 0)),
            pl.BlockSpec((None, 3, 2 * L, 4 * p), lambda i: (i, 0, 0, 0)),
            pl.BlockSpec((None, 2, c, 2 * p), lambda i: (i, 0, 0, 0)),
            pl.BlockSpec((None, 2, c, 2 * p), lambda i: (i, 0, 0, 0)),
            pl.BlockSpec((None, 2, 2 * p), lambda i: (i, 0, 0)),
        ],
        out_specs=pl.BlockSpec((None, rows, L * c), lambda i: (i, 0, 0)),
        out_shape=jax.ShapeDtypeStruct((g, rows, L * c), F32),
        scratch_shapes=[pltpu.VMEM((rows, 4 * p), F32), pltpu.VMEM((rows, 4 * p), F32),
                        pltpu.VMEM((L * c, L * c), BF16), pltpu.VMEM((L * c, 4 * p), BF16),
                        pltpu.VMEM((L * c, 4 * p), BF16), pltpu.VMEM((2 * L * c, 4 * p), F32)],
        compiler_params=_cparams("parallel"),
        name="s5",
    )(ug, pw, bb, cc, a_l)
    return ys.reshape(g, nc, b, L, c).transpose(2, 1, 3, 0, 4).reshape(b * s, w)


def _glu_kernel(ys_ref, u_ref, d_ref, w_ref, o_ref):
    y = ys_ref[...] + u_ref[...] * d_ref[...]
    y = 0.5 * y * (1.0 + jnp.tanh(math.sqrt(2.0 / math.pi) * (y + 0.044715 * (y * y * y))))
    z = jnp.dot(y.astype(BF16), w_ref[...], preferred_element_type=F32)
    o_ref[...] = (y * _sigmoid(z)).astype(o_ref.dtype)


def _s5_glu(ys, proj2, u_off, d_w, glu_w, tm=512):
    t, w = ys.shape
    ub = u_off // w
    return pl.pallas_call(
        _glu_kernel,
        grid=(t // tm,),
        in_specs=[
            pl.BlockSpec((tm, w), lambda i: (i, 0)),
            pl.BlockSpec((tm, w), lambda i: (i, ub)),
            pl.BlockSpec((1, w), lambda i: (0, 0)),
            pl.BlockSpec((w, w), lambda i: (0, 0)),
        ],
        out_specs=pl.BlockSpec((tm, w), lambda i: (i, 0)),
        out_shape=jax.ShapeDtypeStruct((t, w), BF16),
        compiler_params=_cparams("parallel"),
        name="s5_glu",
    )(ys, proj2, d_w.reshape(1, w), glu_w)


def _cross_qk_kernel(wq_ref, km_ref, a_ref, wb):
    @pl.when(pl.program_id(1) == 0)
    def _():
        wb[...] = wq_ref[...].astype(BF16)

    a_ref[...] = lax.dot_general(wb[...], km_ref[...], (((1,), (1,)), ((), ())),
                                 preferred_element_type=F32).astype(a_ref.dtype)


def _cross_vo_kernel(vm_ref, wo_ref, vw_ref, wb):
    @pl.when(pl.program_id(1) == 0)
    def _():
        wb[...] = wo_ref[...].astype(BF16)

    vw_ref[...] = jnp.dot(vm_ref[...], wb[...], preferred_element_type=F32).astype(vw_ref.dtype)


def _cross_fold(kv3, wq3, wo3, layer, n_heads):
    b, nm, d2 = kv3.shape
    d = d2 // 2
    hd = d // n_heads
    a = pl.pallas_call(
        _cross_qk_kernel,
        grid=(n_heads, b),
        in_specs=[pl.BlockSpec((None, d, hd), lambda h, i: (layer, 0, h)),
                  pl.BlockSpec((None, nm, hd), lambda h, i: (i, 0, h))],
        out_specs=pl.BlockSpec((None, d, nm), lambda h, i: (i, 0, h)),
        out_shape=jax.ShapeDtypeStruct((b, d, n_heads * nm), BF16),
        scratch_shapes=[pltpu.VMEM((d, hd), BF16)],
        compiler_params=_cparams("parallel", "arbitrary"),
        name="cross_fold_qk",
    )(wq3, kv3)
    vw = pl.pallas_call(
        _cross_vo_kernel,
        grid=(n_heads, b),
        in_specs=[pl.BlockSpec((None, nm, hd), lambda h, i: (i, 0, n_heads + h)),
                  pl.BlockSpec((None, hd, d), lambda h, i: (layer, h, 0))],
        out_specs=pl.BlockSpec((None, nm, d), lambda h, i: (i, h, 0)),
        out_shape=jax.ShapeDtypeStruct((b, n_heads * nm, d), BF16),
        scratch_shapes=[pltpu.VMEM((hd, d), BF16)],
        compiler_params=_cparams("parallel", "arbitrary"),
        name="cross_fold_vo",
    )(kv3, wo3)
    return a, vw


def _cross_probs_kernel(x_ref, nw_ref, a_ref, p_ref, *, n_heads, scale):
    h = _rmsnorm_rows(x_ref[...], nw_ref[...]).astype(BF16)
    sc = jnp.dot(h, a_ref[...], preferred_element_type=F32) * scale
    nm = sc.shape[1] // n_heads
    for hh in range(n_heads):
        s_h = sc[:, hh * nm:(hh + 1) * nm]
        m = jnp.max(s_h, axis=-1, keepdims=True)
        e = jnp.exp(s_h - m)
        p_ref[:, hh * nm:(hh + 1) * nm] = (e / jnp.sum(e, axis=-1, keepdims=True)).astype(p_ref.dtype)


def _cross_probs(x3, nw, a, n_heads, tq=512):
    b, s, d = x3.shape
    hm = a.shape[2]
    scale = (d // n_heads) ** -0.5
    return pl.pallas_call(
        functools.partial(_cross_probs_kernel, n_heads=n_heads, scale=scale),
        grid=(b, s // tq),
        in_specs=[pl.BlockSpec((None, tq, d), lambda i, j: (i, j, 0)),
                  pl.BlockSpec((1, d), lambda i, j: (0, 0)),
                  pl.BlockSpec((None, d, hm), lambda i, j: (i, 0, 0))],
        out_specs=pl.BlockSpec((None, tq, hm), lambda i, j: (i, j, 0)),
        out_shape=jax.ShapeDtypeStruct((b, s, hm), BF16),
        compiler_params=_cparams("parallel", "parallel"),
        name="cross_probs",
    )(x3, nw.reshape(1, d), a)


def _cross_out_kernel(a_ref, w_ref, r_ref, nw_ref, rw_ref, o_ref, lg_ref):
    x = r_ref[...] + jnp.dot(a_ref[...], w_ref[...], preferred_element_type=F32)
    o_ref[...] = x
    h = _rmsnorm_rows(x, nw_ref[...]).astype(BF16)
    lg_ref[...] = lax.dot_general(rw_ref[...], h, (((1,), (1,)), ((), ())), preferred_element_type=F32)


def _cross_out(a3, w3, res3, nw, rw_t, tm=512):
    b, s, k = a3.shape
    n = w3.shape[2]
    ne = rw_t.shape[0]
    tiles = s // tm
    return pl.pallas_call(
        _cross_out_kernel,
        grid=(b, tiles),
        in_specs=[pl.BlockSpec((None, tm, k), lambda i, j: (i, j, 0)),
                  pl.BlockSpec((None, k, n), lambda i, j: (i, 0, 0)),
                  pl.BlockSpec((None, tm, n), lambda i, j: (i, j, 0)),
                  pl.BlockSpec((1, n), lambda i, j: (0, 0)),
                  pl.BlockSpec((ne, n), lambda i, j: (0, 0))],
        out_specs=[pl.BlockSpec((None, tm, n), lambda i, j: (i, j, 0)),
                   pl.BlockSpec((ne, tm), lambda i, j: (0, i * tiles + j))],
        out_shape=[jax.ShapeDtypeStruct((b, s, n), F32), jax.ShapeDtypeStruct((ne, b * s), F32)],
        compiler_params=_cparams("parallel", "parallel"),
        name="cross_out",
    )(a3, w3, res3, nw.reshape(1, n), rw_t)


def _prefix_sum_lanes(x):
    n = x.shape[-1]
    lane = lax.broadcasted_iota(jnp.int32, x.shape, x.ndim - 1)
    k = 1
    while k < n:
        x = x + jnp.where(lane >= k, pltpu.roll(x, k, x.ndim - 1), 0.0)
        k *= 2
    return x


def _topk_kernel(lg_ref, idx_ref, gate_ref, *, capacity):
    lg = lg_ref[...]
    ne, s = lg.shape
    mx = jnp.max(lg, axis=0, keepdims=True)
    ex = jnp.exp(lg - mx)
    aff = ex / jnp.sum(ex, axis=0, keepdims=True)
    bits = pltpu.bitcast(aff, jnp.int32)

    def bit_body(i, thr):
        cand = thr | jnp.left_shift(jnp.int32(1), 30 - i)
        cnt = jnp.sum((bits >= cand).astype(F32), axis=1, keepdims=True)
        return jnp.where(cnt >= capacity, cand, thr)

    thr = lax.fori_loop(0, 31, bit_body, jnp.zeros((ne, 1), jnp.int32))
    gt = bits > thr
    eq = bits == thr
    n_gt = jnp.sum(gt.astype(F32), axis=1, keepdims=True)
    eq_rank = _prefix_sum_lanes(eq.astype(F32))
    sel = gt | (eq & (eq_rank <= capacity - n_gt))
    slot = _prefix_sum_lanes(sel.astype(F32)) - 1.0
    key = jnp.where(sel, slot, -1.0)
    tok = lax.broadcasted_iota(jnp.int32, (1, s), 1)
    tok_hi = (tok // TOK_SPLIT).astype(BF16)
    tok_lo = (tok % TOK_SPLIT).astype(BF16)
    slots = lax.broadcasted_iota(jnp.int32, (capacity, s), 0).astype(F32)
    for e in range(ne):
        a = aff[e:e + 1, :]
        a_hi = a.astype(BF16)
        r1 = a - a_hi.astype(F32)
        a_mid = r1.astype(BF16)
        a_lo = (r1 - a_mid.astype(F32)).astype(BF16)
        picked = jnp.concatenate([tok_hi, tok_lo, a_hi, a_mid, a_lo, jnp.zeros((3, s), BF16)], axis=0)
        onehot = jnp.where(key[e:e + 1, :] == slots, 1.0, 0.0).astype(BF16)
        res = lax.dot_general(onehot, picked, (((1,), (1,)), ((), ())), preferred_element_type=F32)
        idx_ref[e] = (res[:, 0:1] * TOK_SPLIT + res[:, 1:2]).astype(jnp.int32)
        gate_ref[e] = res[:, 2:3] + res[:, 3:4] + res[:, 4:5]


def _topk(logits, bn, capacity):
    ne, t = logits.shape
    s = t // bn
    assert s <= TOK_SPLIT * 256
    idx, gate = pl.pallas_call(
        functools.partial(_topk_kernel, capacity=capacity),
        grid=(bn,),
        in_specs=[pl.BlockSpec((ne, s), lambda i: (0, i))],
        out_specs=[pl.BlockSpec((None, ne, capacity, 1), lambda i: (i, 0, 0, 0)),
                   pl.BlockSpec((None, ne, capacity, 1), lambda i: (i, 0, 0, 0))],
        out_shape=[jax.ShapeDtypeStruct((bn, ne, capacity, 1), jnp.int32),
                   jax.ShapeDtypeStruct((bn, ne, capacity, 1), F32)],
        compiler_params=_cparams("parallel"),
        name="topk",
    )(logits)
    return idx.reshape(bn, ne, capacity), gate


def _row_copy(src, dst, src_row, dst_row, sem):
    return pltpu.make_async_copy(src.at[pl.ds(src_row, 1), :], dst.at[pl.ds(dst_row, 1), :], sem)


DMA_LOOP_UNROLL = 8


def _near_step(e, b, n_e, n_b, delta):
    lin = (e * n_b + b + delta + n_e * n_b) % (n_e * n_b)
    return lin // n_b, lin % n_b


def _idx_spec(n_e, n_b, cap, delta):
    def index_map(e, b):
        e2, b2 = _near_step(e, b, n_e, n_b, delta)
        return (b2, e2, 0, 0)

    return pl.BlockSpec((None, None, 1, cap), index_map, memory_space=pltpu.SMEM)


def _ffn1_kernel(idx_ref, idx_next_ref, x_hbm, nw_ref, wg_ref, wu_ref, act_ref, xg, sem, *, seq):
    cap = xg.shape[0]
    e, b = pl.program_id(0), pl.program_id(1)
    n_e, n_b = pl.num_programs(0), pl.num_programs(1)
    step = e * n_b + b

    def gather(ir, base):
        def issue(c, carry):
            _row_copy(x_hbm, xg, base + ir[0, c], c, sem).start()
            return carry

        def drain(c, carry):
            _row_copy(x_hbm, xg, base + ir[0, c], c, sem).wait()
            return carry

        return issue, drain

    issue_cur, drain_cur = gather(idx_ref, b * seq)
    _, b_next = _near_step(e, b, n_e, n_b, 1)
    issue_next, drain_next = gather(idx_next_ref, b_next * seq)

    @pl.when(step == 0)
    def _():
        lax.fori_loop(0, cap, issue_cur, 0, unroll=DMA_LOOP_UNROLL)

    lax.fori_loop(0, cap, drain_cur, 0, unroll=DMA_LOOP_UNROLL)
    h = _rmsnorm_rows(xg[...], nw_ref[...]).astype(BF16)

    for c in range(cap):
        issue_next(c, 0)
    g = jnp.dot(h, wg_ref[...], preferred_element_type=F32)
    u = jnp.dot(h, wu_ref[...], preferred_element_type=F32)
    act_ref[...] = ((g * _sigmoid(g)) * u).astype(act_ref.dtype)

    @pl.when(step + 1 == n_e * n_b)
    def _():
        lax.fori_loop(0, cap, drain_next, 0, unroll=DMA_LOOP_UNROLL)


def _ffn1(idx, x2, nw, wg, wu, layer, seq):
    bn, ne, cap = idx.shape
    d = x2.shape[1]
    ff = wg.shape[-1]
    idx4 = idx.reshape(bn, ne, 1, cap)
    return pl.pallas_call(
        functools.partial(_ffn1_kernel, seq=seq),
        grid=(ne, bn),
        in_specs=[
            _idx_spec(ne, bn, cap, 0), _idx_spec(ne, bn, cap, 1),
            pl.BlockSpec(memory_space=pl.ANY),
            pl.BlockSpec((1, d), lambda e, b: (0, 0)),
            pl.BlockSpec((None, None, d, ff), lambda e, b: (layer, e, 0, 0)),
            pl.BlockSpec((None, None, d, ff), lambda e, b: (layer, e, 0, 0)),
        ],
        out_specs=pl.BlockSpec((None, None, cap, ff), lambda e, b: (e, b, 0, 0)),
        out_shape=jax.ShapeDtypeStruct((ne, bn, cap, ff), BF16),
        scratch_shapes=[pltpu.VMEM((cap, d), F32), pltpu.SemaphoreType.DMA(())],
        compiler_params=_cparams("arbitrary", "arbitrary"),
        name="moe_ffn1",
    )(idx4, idx4, x2, nw.reshape(1, d), wg, wu)


def _ffn2_kernel(idx_ref, idx_next_ref, idx_prev_ref, idx_prev2_ref, gate_ref, act_ref, wd_ref, x_in, x_out,
                 xo, ye_sc, sem_in, sem_out, *, seq):
    del x_in
    cap = xo.shape[1]
    e, b = pl.program_id(0), pl.program_id(1)
    n_e, n_b = pl.num_programs(0), pl.num_programs(1)
    step = e * n_b + b
    last = n_e * n_b - 1
    slot = step % 3
    slot_next = (step + 1) % 3
    slot_prev = (step + 2) % 3

    def loop(fn):
        lax.fori_loop(0, cap, fn, 0, unroll=DMA_LOOP_UNROLL)

    def gather(ir, base, sl):
        def copy(c):
            return _row_copy(x_out, xo.at[sl], base + ir[0, c], c, sem_in.at[sl])

        def issue(c, carry):
            copy(c).start()
            return carry

        def drain(c, carry):
            copy(c).wait()
            return carry

        return issue, drain

    def scatter(ir, base, sl):
        def copy(c):
            return _row_copy(xo.at[sl], x_out, c, base + ir[0, c], sem_out.at[sl])

        def issue(c, carry):
            copy(c).start()
            return carry

        def drain(c, carry):
            copy(c).wait()
            return carry

        return issue, drain

    _, b_next = _near_step(e, b, n_e, n_b, 1)
    _, b_prev = _near_step(e, b, n_e, n_b, -1)
    _, b_prev2 = _near_step(e, b, n_e, n_b, -2)
    issue_cur, drain_cur = gather(idx_ref, b * seq, slot)
    issue_next, drain_next = gather(idx_next_ref, b_next * seq, slot_next)
    issue_put, drain_put = scatter(idx_ref, b * seq, slot)
    issue_put_prev, drain_put_prev = scatter(idx_prev_ref, b_prev * seq, slot_prev)
    _, drain_put_prev2 = scatter(idx_prev2_ref, b_prev2 * seq, slot_next)

    @pl.when(step == 0)
    def _():
        loop(issue_cur)

    @pl.when(step >= 2)
    def _():
        loop(drain_put_prev2)

    def expert_out():
        ye_sc[...] = jnp.dot(act_ref[...], wd_ref[...], preferred_element_type=F32) * gate_ref[...]

    @pl.when(step == 0)
    def _():
        for c in range(cap):
            issue_next(c, 0)
        expert_out()

    @pl.when(step > 0)
    def _():
        for c in range(cap):
            issue_put_prev(c, 0)
        for c in range(cap):
            issue_next(c, 0)
        expert_out()

    loop(drain_cur)
    xo[slot] = xo[slot] + ye_sc[...]

    @pl.when(step == last)
    def _():
        loop(issue_put)
        loop(drain_put_prev)
        loop(drain_put)
        loop(drain_next)


def _ffn2(idx, gate, act, wd, layer, x2, seq):
    bn, ne, cap = idx.shape
    assert bn >= 3
    t, d = x2.shape
    ff = wd.shape[2]
    idx4 = idx.reshape(bn, ne, 1, cap)
    return pl.pallas_call(
        functools.partial(_ffn2_kernel, seq=seq),
        grid=(ne, bn),
        in_specs=[
            _idx_spec(ne, bn, cap, 0), _idx_spec(ne, bn, cap, 1), _idx_spec(ne, bn, cap, -1),
            _idx_spec(ne, bn, cap, -2),
            pl.BlockSpec((None, None, cap, 1), lambda e, b: (b, e, 0, 0)),
            pl.BlockSpec((None, None, cap, ff), lambda e, b: (e, b, 0, 0)),
            pl.BlockSpec((None, None, ff, d), lambda e, b: (layer, e, 0, 0)),
            pl.BlockSpec(memory_space=pl.ANY),
        ],
        out_specs=pl.BlockSpec(memory_space=pl.ANY),
        out_shape=jax.ShapeDtypeStruct((t, d), F32),
        input_output_aliases={7: 0},
        scratch_shapes=[pltpu.VMEM((3, cap, d), F32), pltpu.VMEM((cap, d), F32),
                        pltpu.SemaphoreType.DMA((3,)), pltpu.SemaphoreType.DMA((3,))],
        compiler_params=_cparams("arbitrary", "arbitrary"),
        name="moe_ffn2",
    )(idx4, idx4, idx4, idx4, gate, act, wd, x2)


def kernel(x, mem, positions, w_in, w_out, norm_mix_w, norm_cross_w, norm_mem_w, norm_ffn_w, final_norm_w, ret_decay, ret_gn_w, s5_lam_re, s5_lam_im, s5_log_dt, s5_b_re, s5_b_im, s5_c_re, s5_c_im, s5_d, s5_glu_w, cross_wq, cross_wkv, cross_wo, router_w, expert_w_gate, expert_w_up, expert_w_down):
    bn, s, d = x.shape
    depth = w_in.shape[0]
    n_mem = mem.shape[1]
    width_a = 3 * d // 8
    width_b = 3 * d // 8
    width_c = d - width_a - width_b
    n_heads_a = width_a // HEAD_DIM_A
    q_off_b = 3 * width_a
    u_off = q_off_b + 2 * N_HEADS_B * LANES + 2 * width_b
    capacity = EC_CAPACITY_FACTOR * s // N_EXPERTS
    t = bn * s

    ca, sa, cbt, sbt = _rope_tables(positions)
    x2 = x.reshape(t, d)
    mem2 = mem.reshape(bn * n_mem, d)
    w_gate_b = expert_w_gate.astype(BF16)
    w_up_b = expert_w_up.astype(BF16)
    w_down_b = expert_w_down.astype(BF16)
    for l in range(depth):
        h = _rmsnorm(x2, norm_mix_w[l], BF16)
        proj = _matmul(h, w_in, l, F32)
        proj3 = proj.reshape(bn, s, -1)
        a_out = _attn_a(proj3, ca, sa, n_heads_a)
        b_out = _retention(proj3, cbt, sbt, ret_decay[l], ret_gn_w[l], q_off_b, N_HEADS_B)
        s5_tabs = _s5_tables(s5_lam_re[l], s5_lam_im[l], s5_log_dt[l], s5_b_re[l], s5_b_im[l],
                             s5_c_re[l], s5_c_im[l])
        ys = _s5(proj3[:, :, u_off:], s5_tabs)
        c_out = _s5_glu(ys, proj, u_off, s5_d[l], s5_glu_w[l].astype(BF16))
        x2 = _out_proj(a_out.reshape(t, width_a), b_out.reshape(t, width_b), c_out, w_out, l, x2)

        mn = _rmsnorm(mem2, norm_mem_w[l], BF16)
        kv = _matmul(mn, cross_wkv, l, BF16)
        a_fold, vw_fold = _cross_fold(kv.reshape(bn, n_mem, 2 * d), cross_wq, cross_wo, l, N_HEADS_X)
        x3 = x2.reshape(bn, s, d)
        probs = _cross_probs(x3, norm_cross_w[l], a_fold, N_HEADS_X)
        x3, logits = _cross_out(probs, vw_fold, x3, norm_ffn_w[l], router_w[l].T.astype(BF16))
        x2 = x3.reshape(t, d)

        idx, gate = _topk(logits, bn, capacity)
        act = _ffn1(idx, x2, norm_ffn_w[l], w_gate_b, w_up_b, l, s)
        x2 = _ffn2(idx, gate, act, w_down_b, l, x2, s)
    return _rmsnorm(x2, final_norm_w, x.dtype).reshape(bn, s, d)
```
